```python
import jax, jax.numpy as jnp
from jax import lax
import numpy as np

D_MODEL = 1024
BATCH = 8
SEQ = 4096
DEPTH = 1

GRID_W = 64
CTX_LEN = 256
D_MIX = D_MODEL
ML_HEADS = 4
ML_HEAD_DIM = 128
ML_WIDTH = ML_HEADS * ML_HEAD_DIM
HG_HEADS = 4
HG_HEAD_DIM = 128
HG_WIDTH = HG_HEADS * HG_HEAD_DIM
ML_COLS = 4 * ML_WIDTH + 4 * ML_HEADS
HG_COLS = 5 * HG_WIDTH
IN_COLS = ML_COLS + HG_COLS
CONV_W = 3
D_FF = 2816
ML_CHUNK = 64
HG_CHUNK = 32
N_MOD = 6
EPS = 1e-6

kernel_name = "hybrid_mlstm_hgrn2_convffn_prefix_block"

F32 = jnp.float32


def _rmsnorm(x, w):
    xf = x.astype(F32)
    y = xf * lax.rsqrt(jnp.mean(xf * xf, axis=-1, keepdims=True) + EPS)
    return (y * w.astype(F32)).astype(x.dtype)


def _modulate(h, shift, scale):
    return h * (1 + scale) + shift


def _head_rmsnorm(h, w):
    B, H, T, d = h.shape
    y = h * lax.rsqrt(jnp.mean(h * h, axis=-1, keepdims=True) + EPS)
    y = jnp.transpose(y, (0, 2, 1, 3)).reshape(B, T, H * d)
    return y * w.astype(F32)


def _to_heads(a, n_heads):
    B, T, C = a.shape
    return jnp.transpose(a.reshape(B, T, n_heads, C // n_heads), (0, 2, 1, 3)).astype(F32)


def _dwconv2d(x, w, rows, cols):
    B, T, C = x.shape
    img = x.reshape(B, rows, cols, C)
    out = lax.conv_general_dilated(img, w[:, :, None, :].astype(x.dtype), (1, 1), 'SAME',
                                   dimension_numbers=('NHWC', 'HWIO', 'NHWC'), feature_group_count=C)
    return out.reshape(B, T, C)


def _chunks(a, L):
    B, H, T = a.shape[:3]
    a = a.reshape(B, H, T // L, L, *a.shape[3:])
    return jnp.moveaxis(a, 2, 0)


def _unchunks(a):
    nc, B, H, L, d = a.shape
    return jnp.moveaxis(a, 0, 2).reshape(B, H, nc * L, d)


def _mlstm_scan(q, k, v, ig, lf, state):
    L = ML_CHUNK
    causal = jnp.tril(jnp.ones((L, L), bool))

    def step(carry, inp):
        C, n, m = carry
        qc, kc, vc, ic, fc = inp
        b = jnp.cumsum(fc, axis=-1)
        dlog = jnp.where(causal, b[..., :, None] - b[..., None, :] + ic[..., None, :], -jnp.inf)
        inter = b + m[..., None]
        mt = jnp.maximum(inter, jnp.max(dlog, axis=-1))
        w_intra = jnp.exp(dlog - mt[..., None])
        w_inter = jnp.exp(inter - mt)
        s = jnp.einsum('bhtd,bhsd->bhts', qc, kc) * w_intra
        num = jnp.einsum('bhts,bhsv->bhtv', s, vc) + w_inter[..., None] * jnp.einsum('bhtd,bhdv->bhtv', qc, C)
        den = jnp.sum(s, axis=-1) + w_inter * jnp.einsum('bhtd,bhd->bht', qc, n)
        h = num / jnp.maximum(jnp.abs(den), jnp.exp(-mt))[..., None]
        bL = b[..., -1]
        wlog = bL[..., None] - b + ic
        m_new = jnp.maximum(bL + m, jnp.max(wlog, axis=-1))
        decay = jnp.exp(bL + m - m_new)
        ws = jnp.exp(wlog - m_new[..., None])
        C_new = decay[..., None, None] * C + jnp.einsum('bhs,bhsd,bhsv->bhdv', ws, kc, vc)
        n_new = decay[..., None] * n + jnp.einsum('bhs,bhsd->bhd', ws, kc)
        return (C_new, n_new, m_new), h

    state, hs = lax.scan(step, state, tuple(_chunks(a, L) for a in (q, k, v, ig, lf)))
    return _unchunks(hs), state


def _hgrn_scan(q, k, v, lf, S):
    L = HG_CHUNK
    causal = jnp.tril(jnp.ones((L, L), bool))[:, :, None]

    def step(S, inp):
        qc, kc, vc, fc = inp
        b = jnp.cumsum(fc, axis=2)
        rel = jnp.where(causal, b[:, :, :, None, :] - b[:, :, None, :, :], -jnp.inf)
        a = jnp.einsum('bhtk,bhtsk,bhsk->bhts', qc, jnp.exp(rel), kc)
        o = jnp.einsum('bhts,bhsv->bhtv', a, vc) + jnp.einsum('bhtk,bhkv->bhtv', qc * jnp.exp(b), S)
        bL = b[:, :, -1]
        S_new = jnp.exp(bL)[..., None] * S + jnp.einsum('bhsk,bhsv->bhkv', kc * jnp.exp(bL[:, :, None] - b), vc)
        return S_new, o

    S, os_ = lax.scan(step, S, tuple(_chunks(a, L) for a in (q, k, v, lf)))
    return _unchunks(os_), S


def _bidir(scan_fn, ctx_f, ctx_b, lat_f, lat_b, init):
    flip = lambda a: jnp.flip(a, axis=2)
    hc_f, st_f = scan_fn(*ctx_f, init)
    hl_f, _ = scan_fn(*lat_f, st_f)
    hc_b, st_b = scan_fn(*(flip(a) for a in ctx_b), init)
    hl_b, _ = scan_fn(*(flip(a) for a in lat_b), st_b)
    return hl_f + flip(hl_b), hc_f + flip(hc_b)


def _mlstm_prep(p, conv_w, gate_b, rows, cols):
    W, H = ML_WIDTH, ML_HEADS
    qk = jax.nn.silu(_dwconv2d(p[..., :2 * W], conv_w, rows, cols))
    q = _to_heads(qk[..., :W], H) * (ML_HEAD_DIM ** -0.5)
    k = _to_heads(qk[..., W:], H)
    v = _to_heads(p[..., 2 * W:3 * W], H)
    o = p[..., 3 * W:4 * W]
    g = jnp.transpose(p[..., 4 * W:].astype(F32) + gate_b.astype(F32), (0, 2, 1))
    i_f, f_f, i_b, f_b = jnp.split(g, 4, axis=1)
    return q, k, v, o, (i_f, jax.nn.log_sigmoid(f_f)), (i_b, jax.nn.log_sigmoid(f_b))


def _mlstm_mixer(pl, pc, conv_w, gate_b, norm_w, rows, need_ctx):
    ql, kl, vl, ol, gfl, gbl = _mlstm_prep(pl, conv_w, gate_b, rows, GRID_W)
    qc, kc, vc, oc, gfc, gbc = _mlstm_prep(pc, conv_w, gate_b, 1, pc.shape[1])
    B = pl.shape[0]
    init = (jnp.zeros((B, ML_HEADS, ML_HEAD_DIM, ML_HEAD_DIM), F32),
            jnp.zeros((B, ML_HEADS, ML_HEAD_DIM), F32),
            jnp.zeros((B, ML_HEADS), F32))
    hl, hc = _bidir(_mlstm_scan, (qc, kc, vc) + gfc, (qc, kc, vc) + gbc,
                    (ql, kl, vl) + gfl, (ql, kl, vl) + gbl, init)
    out = lambda h, o: (_head_rmsnorm(h, norm_w) * jax.nn.sigmoid(o.astype(F32))).astype(pl.dtype)
    return out(hl, ol), (out(hc, oc) if need_ctx else None)


def _hgrn_prep(p, lb):
    W, H = HG_WIDTH, HG_HEADS
    pf = p.astype(F32)
    q = _to_heads(jax.nn.silu(pf[..., :W]), H) * (HG_HEAD_DIM ** -0.5)
    v = _to_heads(pf[..., 3 * W:4 * W], H)
    g = p[..., 4 * W:5 * W]
    dirs = []
    for j in range(2):
        z = pf[..., (1 + j) * W:(2 + j) * W]
        lbj = lb[j]
        lf = jnp.logaddexp(jnp.log(lbj), jnp.log1p(-lbj) + jax.nn.log_sigmoid(z))
        k = (1 - lbj) * jax.nn.sigmoid(-z)
        dirs.append((q, _to_heads(k, H), v, _to_heads(lf, H)))
    return dirs[0], dirs[1], g


def _hgrn_mixer(pl, pc, lb, norm_w, need_ctx):
    lat_f, lat_b, gl = _hgrn_prep(pl, lb)
    ctx_f, ctx_b, gc = _hgrn_prep(pc, lb)
    B = pl.shape[0]
    init = jnp.zeros((B, HG_HEADS, HG_HEAD_DIM, HG_HEAD_DIM), F32)
    hl, hc = _bidir(_hgrn_scan, ctx_f, ctx_b, lat_f, lat_b, init)
    out = lambda h, g: (_head_rmsnorm(h, norm_w) * jax.nn.silu(g.astype(F32))).astype(pl.dtype)
    return out(hl, gl), (out(hc, gc) if need_ctx else None)


def _conv_ffn(h, w_up, conv_w, w_down, rows, cols):
    u = _dwconv2d(h @ w_up, conv_w, rows, cols)
    a, b = jnp.split(u, 2, axis=-1)
    return (jax.nn.silu(a) * b) @ w_down


def setup_inputs(seed: int = 0) -> dict:
    key = jax.random.key(seed)
    ks = jax.random.split(key, 24)
    nrm = lambda k, shape, s: jax.random.normal(k, shape, F32) * s
    L, D = DEPTH, D_MODEL
    gate_b = jnp.concatenate([
        nrm(ks[8], (L, ML_HEADS), 0.1),
        jnp.linspace(3.0, 6.0, ML_HEADS, dtype=F32)[None] + nrm(ks[9], (L, ML_HEADS), 0.1),
        nrm(ks[10], (L, ML_HEADS), 0.1),
        jnp.linspace(3.0, 6.0, ML_HEADS, dtype=F32)[None] + nrm(ks[11], (L, ML_HEADS), 0.1)], axis=-1)
    return {
        "x": nrm(ks[0], (BATCH, SEQ, D), 1.0),
        "c": nrm(ks[1], (BATCH, D), 1.0),
        "ctx": nrm(ks[2], (BATCH, CTX_LEN, D), 1.0),
        "c_ctx": nrm(ks[3], (D,), 1.0),
        "w_mod": nrm(ks[4], (L, D, N_MOD * D), D ** -0.5),
        "b_mod": nrm(ks[5], (L, N_MOD * D), 0.02),
        "norm1_w": 1.0 + nrm(ks[6], (L, D), 0.05),
        "w_in": nrm(ks[7], (L, D, IN_COLS), D ** -0.5),
        "mlstm_gate_b": gate_b,
        "mlstm_conv_w": nrm(ks[12], (L, CONV_W, CONV_W, 2 * ML_WIDTH), 1.0 / CONV_W),
        "mlstm_norm_w": 1.0 + nrm(ks[13], (L, ML_WIDTH), 0.05),
        "hgrn_lb_logits": nrm(ks[14], (2, L + 1, HG_WIDTH), 0.1),
        "hgrn_norm_w": 1.0 + nrm(ks[15], (L, HG_WIDTH), 0.05),
        "w_out": nrm(ks[16], (L, D_MIX, D), D_MIX ** -0.5),
        "norm2_w": 1.0 + nrm(ks[17], (L, D), 0.05),
        "w_up": nrm(ks[18], (L, D, 2 * D_FF), D ** -0.5),
        "ffn_conv_w": nrm(ks[19], (L, CONV_W, CONV_W, 2 * D_FF), 1.0 / CONV_W),
        "w_down": nrm(ks[20], (L, D_FF, D), D_FF ** -0.5),
        "final_norm_w": 1.0 + nrm(ks[21], (D,), 0.05),
    }


def reference(x, c, ctx, c_ctx, w_mod, b_mod, norm1_w, w_in, mlstm_gate_b, mlstm_conv_w, mlstm_norm_w,
              hgrn_lb_logits, hgrn_norm_w, w_out, norm2_w, w_up, ffn_conv_w, w_down, final_norm_w):
    rows = x.shape[1] // GRID_W
    ctx_len = ctx.shape[1]
    lower_bounds = jnp.cumsum(jax.nn.softmax(hgrn_lb_logits.astype(F32), axis=1), axis=1)
    xl, xc = x, ctx
    for l in range(DEPTH):
        last = l == DEPTH - 1
        mod_l = (jax.nn.silu(c) @ w_mod[l] + b_mod[l])[:, None, :]
        mod_c = (jax.nn.silu(c_ctx) @ w_mod[l] + b_mod[l])[None, None, :]
        sh1, sc1, g1, sh2, sc2, g2 = jnp.split(mod_l, N_MOD, axis=-1)
        sh1c, sc1c, g1c, sh2c, sc2c, g2c = jnp.split(mod_c, N_MOD, axis=-1)
        pl = _modulate(_rmsnorm(xl, norm1_w[l]), sh1, sc1) @ w_in[l]
        pc = _modulate(_rmsnorm(xc, norm1_w[l]), sh1c, sc1c) @ w_in[l]
        ml_l, ml_c = _mlstm_mixer(pl[..., :ML_COLS], pc[..., :ML_COLS], mlstm_conv_w[l], mlstm_gate_b[l],
                                  mlstm_norm_w[l], rows, not last)
        hg_l, hg_c = _hgrn_mixer(pl[..., ML_COLS:], pc[..., ML_COLS:], lower_bounds[:, l], hgrn_norm_w[l], not last)
        xl = xl + g1 * (jnp.concatenate([ml_l, hg_l], axis=-1) @ w_out[l])
        xl = xl + g2 * _conv_ffn(_modulate(_rmsnorm(xl, norm2_w[l]), sh2, sc2), w_up[l], ffn_conv_w[l],
                                 w_down[l], rows, GRID_W)
        if not last:
            xc = xc + g1c * (jnp.concatenate([ml_c, hg_c], axis=-1) @ w_out[l])
            xc = xc + g2c * _conv_ffn(_modulate(_rmsnorm(xc, norm2_w[l]), sh2c, sc2c), w_up[l], ffn_conv_w[l],
                                      w_down[l], 1, ctx_len)
    return _rmsnorm(xl, final_norm_w)
```

```python
import functools

import numpy as np
import jax
import jax.numpy as jnp
from jax import lax
from jax.experimental import pallas as pl
from jax.experimental.pallas import tpu as pltpu

F32 = jnp.float32
BF16 = jnp.bfloat16
HIGHEST = lax.Precision.HIGHEST

EPS = 1e-6
GRID_W = 64
HEAD_DIM = 128
N_HEADS = 4
MIX_W = N_HEADS * HEAD_DIM
N_MOD = 6
GATE_PAD = 128
V7X_VMEM_LIMIT = 56 * 1024 * 1024

ML_CHUNK = 64
HG_CHUNK = 32
HG_BLOCK = 256
TOK_TILE = 512
FFN_ROWS = 8
FFN_CW = 256


def _sigmoid(x):
    return 1.0 / (1.0 + jnp.exp(-x))


def _log_sigmoid(x):
    return jnp.minimum(x, 0.0) - jnp.log1p(jnp.exp(-jnp.abs(x)))


def _tdot(a, b):
    return jnp.dot(a.T.astype(BF16), b.astype(BF16), preferred_element_type=F32)


def _dot_nt(a, b):
    return lax.dot_general(a, b, (((1,), (1,)), ((), ())), preferred_element_type=F32)


def _mod_kernel(c_ref, w_ref, b_ref, o_ref):
    c = c_ref[...]
    s = (c * _sigmoid(c)).astype(BF16)
    o_ref[...] = jnp.dot(s, w_ref[...].astype(BF16), preferred_element_type=F32) + b_ref[...]


def _modulation(cc, w_mod, b_mod):
    rows, d = cc.shape
    n = w_mod.shape[1]
    tn = 1024
    return pl.pallas_call(
        _mod_kernel,
        out_shape=jax.ShapeDtypeStruct((rows, n), F32),
        grid=(n // tn,),
        in_specs=[pl.BlockSpec((rows, d), lambda j: (0, 0)),
                  pl.BlockSpec((d, tn), lambda j: (0, j)),
                  pl.BlockSpec((1, tn), lambda j: (0, j))],
        out_specs=pl.BlockSpec((rows, tn), lambda j: (0, j)),
        compiler_params=pltpu.CompilerParams(dimension_semantics=("arbitrary",)),
        name="mod",
    )(cc, w_mod, b_mod)


def _in_proj_kernel(x_ref, sh_ref, sc_ref, nw_ref, w_ref, pa_ref, pb_ref, pz_ref, pg_ref):
    x = x_ref[0]
    ms = jnp.mean(x * x, axis=-1, keepdims=True)
    h = x * lax.rsqrt(ms + EPS) * nw_ref[...]
    h = (h * (1.0 + sc_ref[0]) + sh_ref[0]).astype(BF16)
    col = 0
    for ref, dt in ((pa_ref, BF16), (pb_ref, BF16), (pz_ref, F32), (pg_ref, F32)):
        width = ref.shape[-1]
        step = min(width, 512)
        for j in range(width // step):
            y = jnp.dot(h, w_ref[:, col:col + step], preferred_element_type=F32)
            ref[0, :, j * step:(j + 1) * step] = y.astype(dt)
            col += step


def _in_proj(x, sh, sc, norm_w, w_cat, tm):
    b, t, d = x.shape
    nc = w_cat.shape[1]
    widths = (4 * MIX_W, 3 * MIX_W, 2 * MIX_W, GATE_PAD)
    assert sum(widths) == nc
    mod_idx = (lambda bi, i: (bi, 0, 0)) if sh.shape[0] > 1 else (lambda bi, i: (0, 0, 0))
    out_shape = [jax.ShapeDtypeStruct((b, t, widths[0]), BF16),
                 jax.ShapeDtypeStruct((b, t, widths[1]), BF16),
                 jax.ShapeDtypeStruct((b, t, widths[2]), F32),
                 jax.ShapeDtypeStruct((b, t, widths[3]), F32)]
    return pl.pallas_call(
        _in_proj_kernel,
        out_shape=out_shape,
        grid=(b, t // tm),
        in_specs=[pl.BlockSpec((1, tm, d), lambda bi, i: (bi, i, 0)),
                  pl.BlockSpec((1, 1, d), mod_idx),
                  pl.BlockSpec((1, 1, d), mod_idx),
                  pl.BlockSpec((1, d), lambda bi, i: (0, 0)),
                  pl.BlockSpec((d, nc), lambda bi, i: (0, 0))],
        out_specs=[pl.BlockSpec((1, tm, w), lambda bi, i: (bi, i, 0)) for w in widths],
        compiler_params=pltpu.CompilerParams(dimension_semantics=("arbitrary", "arbitrary"),
                                             vmem_limit_bytes=V7X_VMEM_LIMIT),
        name="in_proj",
    )(x, sh, sc, norm_w, w_cat)


def _conv3x3(u, w9, width, rows_out, two_d):
    ntok = u.shape[0]
    col = lax.broadcasted_iota(jnp.int32, (ntok, 1), 0) % width
    left = jnp.where(col == 0, 0.0, pltpu.roll(u, 1, 0))
    right = jnp.where(col == width - 1, 0.0, pltpu.roll(u, ntok - 1, 0))
    srcs = (left, u, right)
    acc = None
    for dr in (range(3) if two_d else (1,)):
        lo = dr * width if two_d else 0
        for dc in range(3):
            term = srcs[dc][lo:lo + rows_out * width] * w9[dr * 3 + dc:dr * 3 + dc + 1, :]
            acc = term if acc is None else acc + term
    return acc


def _halo_specs(rows_per_blk, width, n_rows, chan_blk):
    prev = pl.BlockSpec((1, width, chan_blk),
                        lambda bi, i, c: (bi, jnp.maximum(i * rows_per_blk - 1, 0), c))
    nxt = pl.BlockSpec((1, width, chan_blk),
                       lambda bi, i, c: (bi, jnp.minimum((i + 1) * rows_per_blk, n_rows - 1), c))
    return prev, nxt


def _qk_conv_kernel(*refs, width, rows_per_blk, two_d):
    if two_d:
        main_ref, prev_ref, next_ref, w_ref, o_ref = refs
        i = pl.program_id(1)
        last = pl.num_programs(1) - 1
        prev = jnp.where(i > 0, prev_ref[0].astype(F32), 0.0)
        nxt = jnp.where(i < last, next_ref[0].astype(F32), 0.0)
        u = jnp.concatenate([prev, main_ref[0].astype(F32), nxt], axis=0)
    else:
        main_ref, w_ref, o_ref = refs
        u = main_ref[0].astype(F32)
    y = _conv3x3(u, w_ref[...], width, rows_per_blk, two_d)
    y = y * _sigmoid(y)
    scale = jnp.where(pl.program_id(2) == 0, HEAD_DIM ** -0.5, 1.0)
    o_ref[0] = (y * scale).astype(BF16)


def _qk_conv(pa, w9, width, two_d):
    b, t, _ = pa.shape
    cb = MIX_W
    if two_d:
        rows_per_blk = 8
        n_rows = t // width
        grid = (b, n_rows // rows_per_blk, 2)
        prev, nxt = _halo_specs(rows_per_blk, width, n_rows, cb)
        in_specs = [pl.BlockSpec((1, rows_per_blk * width, cb), lambda bi, i, c: (bi, i, c)), prev, nxt]
        args = (pa, pa, pa)
    else:
        rows_per_blk = 1
        assert t == width
        grid = (b, 1, 2)
        in_specs = [pl.BlockSpec((1, width, cb), lambda bi, i, c: (bi, 0, c))]
        args = (pa,)
    in_specs.append(pl.BlockSpec((9, cb), lambda bi, i, c: (0, c)))
    return pl.pallas_call(
        functools.partial(_qk_conv_kernel, width=width, rows_per_blk=rows_per_blk, two_d=two_d),
        out_shape=jax.ShapeDtypeStruct((b, t, 2 * MIX_W), BF16),
        grid=grid,
        in_specs=in_specs,
        out_specs=pl.BlockSpec((1, rows_per_blk * width, cb), lambda bi, i, c: (bi, i, c)),
        compiler_params=pltpu.CompilerParams(dimension_semantics=("arbitrary",) * 3,
                                             vmem_limit_bytes=V7X_VMEM_LIMIT),
        name="qk_conv",
    )(*args, w9)


def _mlstm_kernel(*refs, chunk, need_h):
    ins = refs[:13]
    q_refs, k_refs, v_refs, g_refs = ins[0:2], ins[2:4], ins[4:6], ins[6:8]
    gbias_ref, cum_ref, c0_ref, n0_ref, m0_ref = ins[8:13]
    outs = refs[13:]
    if need_h:
        h_refs, (c_ref, n_ref, m_ref) = outs[0:2], outs[2:5]
    else:
        c_ref, n_ref, m_ref = outs

    @pl.when(pl.program_id(1) == 0)
    def _():
        c_ref[...] = c0_ref[...]
        n_ref[...] = n0_ref[...]
        m_ref[...] = m0_ref[...]

    L = chunk
    t_idx = lax.broadcasted_iota(jnp.int32, (L, L), 0)
    s_idx = lax.broadcasted_iota(jnp.int32, (L, L), 1)
    for d in range(2):
        mask = (s_idx <= t_idx) if d == 0 else (s_idx >= t_idx)
        last = L - 1 if d == 0 else 0
        g = g_refs[d][0] + gbias_ref[...]
        lf = _log_sigmoid(g)
        bc = jnp.dot(cum_ref[d], lf, precision=HIGHEST, preferred_element_type=F32)
        ut = (g - pltpu.roll(bc, GATE_PAD - N_HEADS, 1)).T
        for h in range(N_HEADS):
            idx = d * N_HEADS + h
            ic, fc = 2 * N_HEADS * d + h, 2 * N_HEADS * d + N_HEADS + h
            hs = slice(h * HEAD_DIM, (h + 1) * HEAD_DIM)
            b_col = bc[:, fc:fc + 1]
            i_col = g[:, ic:ic + 1]
            b_last = bc[last:last + 1, fc:fc + 1]
            m_old = m_ref[0, idx:idx + 1, 0:1]
            c_old = c_ref[0, idx]
            n_old = n_ref[0, idx:idx + 1, :]
            qh = q_refs[d][0, :, hs]
            kh = k_refs[d][0, :, hs]
            vh = v_refs[d][0, :, hs]
            if need_h:
                dlog = jnp.where(mask, b_col + ut[ic:ic + 1, :], -jnp.inf)
                inter = b_col + m_old
                mt = jnp.maximum(inter, jnp.max(dlog, axis=-1, keepdims=True))
                w_intra = jnp.exp(dlog - mt)
                w_inter = jnp.exp(inter - mt)
                s = _dot_nt(qh, kh) * w_intra
                num = (jnp.dot(s.astype(BF16), vh, preferred_element_type=F32)
                       + w_inter * jnp.dot(qh, c_old.astype(BF16), preferred_element_type=F32))
                den = (jnp.sum(s, axis=-1, keepdims=True)
                       + w_inter * jnp.sum(qh.astype(F32) * n_old, axis=-1, keepdims=True))
                hh = num / jnp.maximum(jnp.abs(den), jnp.exp(-mt))
                h_refs[d][0, :, hs] = hh.astype(h_refs[d].dtype)
            wlog = b_last - b_col + i_col
            m_new = jnp.maximum(b_last + m_old, jnp.max(wlog, axis=0, keepdims=True))
            decay = jnp.exp(b_last + m_old - m_new)
            kw = kh.astype(F32) * jnp.exp(wlog - m_new)
            c_ref[0, idx] = decay * c_old + _tdot(kw, vh)
            n_ref[0, idx:idx + 1, :] = decay * n_old + jnp.sum(kw, axis=0, keepdims=True)
            m_ref[0, idx:idx + 1, :] = jnp.broadcast_to(m_new, (1, HEAD_DIM))


def _cum_masks(chunk):
    t = np.arange(chunk)
    fwd = (t[None, :] <= t[:, None]).astype(np.float32)
    return jnp.asarray(np.stack([fwd, fwd.T]))


def _mlstm_scan(qk, pa, pg, gate_b, state, need_h):
    b, t, _ = qk.shape
    L = ML_CHUNK
    nc = t // L
    fwd = lambda col: (lambda bi, j: (bi, j, col))
    bwd = lambda col: (lambda bi, j: (bi, nc - 1 - j, col))
    blk = lambda w, im: pl.BlockSpec((1, L, w), im)
    whole = lambda shape: pl.BlockSpec(shape, lambda bi, j: (0,) * len(shape))
    st_c = pl.BlockSpec((1, 2 * N_HEADS, HEAD_DIM, HEAD_DIM), lambda bi, j: (bi, 0, 0, 0))
    st_v = pl.BlockSpec((1, 2 * N_HEADS, HEAD_DIM), lambda bi, j: (bi, 0, 0))
    in_specs = [blk(MIX_W, fwd(0)), blk(MIX_W, bwd(0)),
                blk(MIX_W, fwd(1)), blk(MIX_W, bwd(1)),
                blk(MIX_W, fwd(2)), blk(MIX_W, bwd(2)),
                blk(GATE_PAD, fwd(0)), blk(GATE_PAD, bwd(0)),
                whole((1, GATE_PAD)), whole((2, L, L)), st_c, st_v, st_v]
    state_shapes = [jax.ShapeDtypeStruct((b, 2 * N_HEADS, HEAD_DIM, HEAD_DIM), F32),
                    jax.ShapeDtypeStruct((b, 2 * N_HEADS, HEAD_DIM), F32),
                    jax.ShapeDtypeStruct((b, 2 * N_HEADS, HEAD_DIM), F32)]
    out_shape, out_specs = state_shapes, [st_c, st_v, st_v]
    if need_h:
        out_shape = [jax.ShapeDtypeStruct((b, t, MIX_W), BF16)] * 2 + out_shape
        out_specs = [blk(MIX_W, fwd(0)), blk(MIX_W, bwd(0))] + out_specs
    res = pl.pallas_call(
        functools.partial(_mlstm_kernel, chunk=L, need_h=need_h),
        out_shape=out_shape,
        grid=(b, nc),
        in_specs=in_specs,
        out_specs=out_specs,
        compiler_params=pltpu.CompilerParams(dimension_semantics=("arbitrary", "arbitrary"),
                                             vmem_limit_bytes=V7X_VMEM_LIMIT),
        name="mlstm_lat" if need_h else "mlstm_ctx",
    )(qk, qk, qk, qk, pa, pa, pg, pg, gate_b, _cum_masks(L), *state)
    return (res[0], res[1], tuple(res[2:])) if need_h else (None, None, tuple(res))


def _hgrn_sum_masks(chunk):
    L = chunk
    t = np.arange(L)[:, None]
    r = np.arange(L)[None, :]
    n_lev = int(np.log2(L))
    fwd = [(r <= t), (r > t)]
    for lev in range(n_lev):
        w = 1 << lev
        ref = (t // (2 * w)) * 2 * w + w - 1
        is_q = (t // w) % 2 == 1
        fwd.append(np.where(is_q, (r > ref) & (r <= t), (r > t) & (r <= ref)))
    fwd = np.concatenate([m.astype(np.float32) for m in fwd], axis=0)
    bwd = fwd.reshape(n_lev + 2, L, L)[:, ::-1, ::-1].reshape(fwd.shape)
    pair = np.stack([((t // (2 << lev)) == (r // (2 << lev))) for lev in range(n_lev)]).astype(np.float32)
    return jnp.asarray(np.stack([fwd, bwd])), jnp.asarray(pair)


def _hgrn_kernel(*refs, chunk, block, need_o):
    q_refs, v_refs, z_refs = refs[0:2], refs[2:4], refs[4:6]
    lbl_ref, sums_ref, pair_ref, s0_ref = refs[6:10]
    outs = refs[10:]
    if need_o:
        o_refs, s_ref = outs[0:2], outs[2]
    else:
        (s_ref,) = outs

    @pl.when(pl.program_id(1) == 0)
    def _():
        s_ref[...] = s0_ref[...]

    L = chunk
    n_lev = L.bit_length() - 1
    n_chunks = block // L
    tok = lax.broadcasted_iota(jnp.int32, (L, 1), 0)

    for d in range(2):
        lg = lbl_ref[d]
        e = jnp.exp(lg - jnp.max(lg, axis=0, keepdims=True))
        lb = e[0:1, :] / jnp.sum(e, axis=0, keepdims=True)
        last = L - 1 if d == 0 else 0

        def body(c, carry, d=d, lb=lb, last=last):
            cc = c if d == 0 else n_chunks - 1 - c
            rows = pl.ds(pl.multiple_of(cc * L, L), L)
            z = z_refs[d][0, rows, :]
            kk = (1.0 - lb) / (1.0 + jnp.exp(z))
            lf = jnp.log1p(-kk)
            sums = jnp.dot(sums_ref[d], lf, precision=HIGHEST, preferred_element_type=F32)
            v = v_refs[d][0, rows, :]
            if need_o:
                qa = q_refs[d][0, rows, :].astype(F32)
                qs = qa * _sigmoid(qa) * (HEAD_DIM ** -0.5)
                vf = v.astype(F32)
            for h in range(N_HEADS):
                idx = d * N_HEADS + h
                hs = slice(h * HEAD_DIM, (h + 1) * HEAD_DIM)
                b = sums[0:L, hs]
                k = kk[:, hs]
                st_old = s_ref[0, idx]
                if need_o:
                    q = qs[:, hs]
                    a = jnp.zeros((L, L), F32)
                    for lev in range(n_lev):
                        w = 1 << lev
                        is_q = ((tok // w) % 2) == (1 if d == 0 else 0)
                        ex = jnp.exp(sums[(2 + lev) * L:(3 + lev) * L, hs])
                        ql = jnp.where(is_q, q * ex, 0.0).astype(BF16)
                        kl = jnp.where(is_q, 0.0, k * ex).astype(BF16)
                        a = a + pair_ref[lev] * _dot_nt(ql, kl)
                    o = (jnp.dot(a.astype(BF16), v[:, hs], preferred_element_type=F32)
                         + jnp.sum(q * k, axis=-1, keepdims=True) * vf[:, hs]
                         + _dot_nt((q * jnp.exp(b)).astype(BF16), st_old.astype(BF16)))
                    o_refs[d][0, rows, hs] = o.astype(o_refs[d].dtype)
                kd = k * jnp.exp(sums[L:2 * L, hs])
                s_ref[0, idx] = st_old * jnp.exp(b[last:last + 1, :]) + _tdot(v[:, hs].astype(F32), kd)
            return carry

        lax.fori_loop(0, n_chunks, body, 0)


def _hgrn_scan(pb, pz, lb_logits, state, need_o):
    b, t, _ = pb.shape
    L, blk_t = HG_CHUNK, min(HG_BLOCK, t)
    nb = t // blk_t
    fwd = lambda col: (lambda bi, j: (bi, j, col))
    bwd = lambda col: (lambda bi, j: (bi, nb - 1 - j, col))
    blk = lambda im: pl.BlockSpec((1, blk_t, MIX_W), im)
    whole = lambda shape: pl.BlockSpec(shape, lambda bi, j: (0,) * len(shape))
    sums, pair = _hgrn_sum_masks(L)
    st = pl.BlockSpec((1, 2 * N_HEADS, HEAD_DIM, HEAD_DIM), lambda bi, j: (bi, 0, 0, 0))
    in_specs = [blk(fwd(0)), blk(bwd(0)),
                blk(fwd(1)), blk(bwd(1)),
                blk(fwd(0)), blk(bwd(1)),
                whole(lb_logits.shape), whole(sums.shape), whole(pair.shape), st]
    st_shape = jax.ShapeDtypeStruct((b, 2 * N_HEADS, HEAD_DIM, HEAD_DIM), F32)
    out_shape, out_specs = [st_shape], [st]
    if need_o:
        out_shape = [jax.ShapeDtypeStruct((b, t, MIX_W), BF16)] * 2 + out_shape
        out_specs = [blk(fwd(0)), blk(bwd(0))] + out_specs
    res = pl.pallas_call(
        functools.partial(_hgrn_kernel, chunk=L, block=blk_t, need_o=need_o),
        out_shape=out_shape,
        grid=(b, nb),
        in_specs=in_specs,
        out_specs=out_specs,
        compiler_params=pltpu.CompilerParams(dimension_semantics=("arbitrary", "arbitrary"),
                                             vmem_limit_bytes=V7X_VMEM_LIMIT),
        name="hgrn_lat" if need_o else "hgrn_ctx",
    )(pb, pb, pb, pb, pz, pz, lb_logits, sums, pair, state)
    return (res[0], res[1], res[2]) if need_o else (None, None, res[0])


def _head_norm(hsum, w):
    parts = []
    for h in range(N_HEADS):
        blk = hsum[:, h * HEAD_DIM:(h + 1) * HEAD_DIM]
        parts.append(blk * lax.rsqrt(jnp.mean(blk * blk, axis=-1, keepdims=True) + EPS))
    return jnp.concatenate(parts, axis=-1) * w


def _out_proj_kernel(x_ref, mf_ref, mb_ref, o_ref, hf_ref, hb_ref, g_ref, mw_ref, hw_ref, wout_ref,
                     g1_ref, sh2_ref, sc2_ref, n2w_ref, x1_ref, h2_ref):
    ml = _head_norm(mf_ref[0].astype(F32) + mb_ref[0].astype(F32), mw_ref[...])
    ml = ml * _sigmoid(o_ref[0].astype(F32))
    hg = _head_norm(hf_ref[0].astype(F32) + hb_ref[0].astype(F32), hw_ref[...])
    gg = g_ref[0].astype(F32)
    hg = hg * (gg * _sigmoid(gg))
    mix = (jnp.dot(ml.astype(BF16), wout_ref[0:MIX_W, :], preferred_element_type=F32)
           + jnp.dot(hg.astype(BF16), wout_ref[MIX_W:2 * MIX_W, :], preferred_element_type=F32))
    x1 = x_ref[0] + g1_ref[0] * mix
    x1_ref[0] = x1
    y = x1 * lax.rsqrt(jnp.mean(x1 * x1, axis=-1, keepdims=True) + EPS) * n2w_ref[...]
    h2_ref[0] = (y * (1.0 + sc2_ref[0]) + sh2_ref[0]).astype(BF16)


def _out_proj(x, mf, mb, pa, hf, hb, pb, ml_norm_w, hg_norm_w, w_out, g1, sh2, sc2, norm2_w, tm):
    b, t, d = x.shape
    tok = lambda w, col: pl.BlockSpec((1, tm, w), lambda bi, i: (bi, i, col))
    per_b = pl.BlockSpec((1, 1, d), lambda bi, i: (bi, 0, 0))
    whole = lambda shape: pl.BlockSpec(shape, lambda bi, i: (0,) * len(shape))
    return pl.pallas_call(
        _out_proj_kernel,
        out_shape=[jax.ShapeDtypeStruct((b, t, d), F32), jax.ShapeDtypeStruct((b, t, d), BF16)],
        grid=(b, t // tm),
        in_specs=[tok(d, 0), tok(MIX_W, 0), tok(MIX_W, 0), tok(MIX_W, 3),
                  tok(MIX_W, 0), tok(MIX_W, 0), tok(MIX_W, 2),
                  whole((1, MIX_W)), whole((1, MIX_W)), whole(w_out.shape),
                  per_b, per_b, per_b, whole((1, d))],
        out_specs=[tok(d, 0), tok(d, 0)],
        compiler_params=pltpu.CompilerParams(dimension_semantics=("arbitrary", "arbitrary"),
                                             vmem_limit_bytes=V7X_VMEM_LIMIT),
        name="out_proj",
    )(x, mf, mb, pa, hf, hb, pb, ml_norm_w, hg_norm_w, w_out, g1, sh2, sc2, norm2_w)


def _ffn_kernel(h_ref, hp_ref, hn_ref, x1_ref, g2_ref, wup_ref, cw_ref, wdn_ref, fw_ref, o_ref,
                *, width, rows_per_blk, chan_chunk):
    i = pl.program_id(1)
    last = pl.num_programs(1) - 1
    hp = jnp.where(i > 0, hp_ref[0], jnp.zeros_like(hp_ref[0]))
    hn = jnp.where(i < last, hn_ref[0], jnp.zeros_like(hn_ref[0]))
    hin = jnp.concatenate([hp, h_ref[0], hn], axis=0)
    d_ff = wdn_ref.shape[0]
    acc = None
    for c in range(d_ff // chan_chunk):
        halves = []
        for base in (0, d_ff):
            cols = slice(base + c * chan_chunk, base + (c + 1) * chan_chunk)
            u = jnp.dot(hin, wup_ref[:, cols], preferred_element_type=F32)
            halves.append(_conv3x3(u, cw_ref[:, cols], width, rows_per_blk, True))
        a, bgate = halves
        act = (a * _sigmoid(a) * bgate).astype(BF16)
        part = jnp.dot(act, wdn_ref[c * chan_chunk:(c + 1) * chan_chunk, :], preferred_element_type=F32)
        acc = part if acc is None else acc + part
    x2 = x1_ref[0] + g2_ref[0] * acc
    o_ref[0] = x2 * lax.rsqrt(jnp.mean(x2 * x2, axis=-1, keepdims=True) + EPS) * fw_ref[...]


def _conv_ffn(h2, x1, g2, w_up, conv_w9, w_down, final_w):
    b, t, d = x1.shape
    width, rows_per_blk = GRID_W, FFN_ROWS
    n_rows = t // width
    tm = rows_per_blk * width
    prev = pl.BlockSpec((1, width, d), lambda bi, i: (bi, jnp.maximum(i * rows_per_blk - 1, 0), 0))
    nxt = pl.BlockSpec((1, width, d),
                       lambda bi, i: (bi, jnp.minimum((i + 1) * rows_per_blk, n_rows - 1), 0))
    tok = pl.BlockSpec((1, tm, d), lambda bi, i: (bi, i, 0))
    whole = lambda shape: pl.BlockSpec(shape, lambda bi, i: (0,) * len(shape),
                                       pipeline_mode=pl.Buffered(1))
    return pl.pallas_call(
        functools.partial(_ffn_kernel, width=width, rows_per_blk=rows_per_blk, chan_chunk=FFN_CW),
        out_shape=jax.ShapeDtypeStruct((b, t, d), F32),
        grid=(b, n_rows // rows_per_blk),
        in_specs=[tok, prev, nxt, tok, pl.BlockSpec((1, 1, d), lambda bi, i: (bi, 0, 0)),
                  whole(w_up.shape), whole(conv_w9.shape), whole(w_down.shape), whole((1, d))],
        out_specs=tok,
        compiler_params=pltpu.CompilerParams(dimension_semantics=("arbitrary", "arbitrary"),
                                             vmem_limit_bytes=V7X_VMEM_LIMIT),
        name="conv_ffn",
    )(h2, h2, h2, x1, g2, w_up, conv_w9, w_down, final_w)


def _reorder_w_in(w_in):
    d = w_in.shape[0]
    ml, hg = 4 * MIX_W, 4 * MIX_W + 4 * N_HEADS
    gates = jnp.pad(w_in[:, ml:hg], ((0, 0), (0, GATE_PAD - 4 * N_HEADS)))
    cols = [w_in[:, :ml],
            w_in[:, hg:hg + MIX_W], w_in[:, hg + 3 * MIX_W:hg + 5 * MIX_W],
            w_in[:, hg + MIX_W:hg + 3 * MIX_W], gates]
    return jnp.concatenate(cols, axis=1).astype(BF16)


def kernel(x, c, ctx, c_ctx, w_mod, b_mod, norm1_w, w_in, mlstm_gate_b, mlstm_conv_w, mlstm_norm_w,
           hgrn_lb_logits, hgrn_norm_w, w_out, norm2_w, w_up, ffn_conv_w, w_down, final_norm_w):
    b, t, d = x.shape
    assert w_mod.shape[0] == 1 and hgrn_lb_logits.shape[1] == 2, "single-layer block"
    assert d == 2 * MIX_W and t % (GRID_W * FFN_ROWS) == 0

    rows = -(-(b + 1) // 8) * 8
    cc = jnp.concatenate([c, c_ctx[None, :], jnp.zeros((rows - b - 1, d), F32)], axis=0)
    mod = _modulation(cc, w_mod[0], b_mod)
    mod_l = mod[:b].reshape(b, N_MOD, 1, d)
    sh1, sc1, g1, sh2, sc2, g2 = (mod_l[:, j] for j in range(N_MOD))
    mod_c = mod[b:b + 1].reshape(1, N_MOD, 1, d)
    sh1c, sc1c = mod_c[:, 0], mod_c[:, 1]

    w_cat = _reorder_w_in(w_in[0])
    pa, pb, pz, pg = _in_proj(x, sh1, sc1, norm1_w, w_cat, TOK_TILE)
    pa_c, pb_c, pz_c, pg_c = _in_proj(ctx, sh1c, sc1c, norm1_w, w_cat, ctx.shape[1])

    conv9 = mlstm_conv_w[0].reshape(9, 2 * MIX_W)
    qk = _qk_conv(pa, conv9, GRID_W, True)
    qk_c = _qk_conv(pa_c, conv9, ctx.shape[1], False)

    gate_b = jnp.pad(mlstm_gate_b, ((0, 0), (0, GATE_PAD - mlstm_gate_b.shape[1])))
    zero_state = (jnp.zeros((b, 2 * N_HEADS, HEAD_DIM, HEAD_DIM), F32),
                  jnp.zeros((b, 2 * N_HEADS, HEAD_DIM), F32),
                  jnp.zeros((b, 2 * N_HEADS, HEAD_DIM), F32))
    _, _, ml_state = _mlstm_scan(qk_c, pa_c, pg_c, gate_b, zero_state, False)
    mf, mb, _ = _mlstm_scan(qk, pa, pg, gate_b, ml_state, True)

    _, _, hg_state = _hgrn_scan(pb_c, pz_c, hgrn_lb_logits, zero_state[0], False)
    hf, hb, _ = _hgrn_scan(pb, pz, hgrn_lb_logits, hg_state, True)

    x1, h2 = _out_proj(x, mf, mb, pa, hf, hb, pb, mlstm_norm_w, hgrn_norm_w, w_out[0].astype(BF16),
                       g1, sh2, sc2, norm2_w, TOK_TILE)
    return _conv_ffn(h2, x1, g2, w_up[0].astype(BF16), ffn_conv_w[0].reshape(9, -1),
                     w_down[0].astype(BF16), final_norm_w[None, :])
```

```python
import functools

import numpy as np
import jax
import jax.numpy as jnp
from jax import lax
from jax.experimental import pallas as pl
from jax.experimental.pallas import tpu as pltpu

F32 = jnp.float32
BF16 = jnp.bfloat16
HIGHEST = lax.Precision.HIGHEST

EPS = 1e-6
GRID_W = 64
HEAD_DIM = 128
N_HEADS = 4
MIX_W = N_HEADS * HEAD_DIM
N_MOD = 6
GATE_PAD = 128
V7X_VMEM_LIMIT = 56 * 1024 * 1024

ML_CHUNK = 64
HG_CHUNK = 32
HG_BLOCK = 256
TOK_TILE = 512
FFN_ROWS = 8
FFN_CW = 256


def _sigmoid(x):
    return 1.0 / (1.0 + jnp.exp(-x))


def _log_sigmoid(x):
    return jnp.minimum(x, 0.0) - jnp.log1p(jnp.exp(-jnp.abs(x)))


def _tdot(a, b):
    return jnp.dot(a.T.astype(BF16), b.astype(BF16), preferred_element_type=F32)


def _dot_nt(a, b):
    return lax.dot_general(a, b, (((1,), (1,)), ((), ())), preferred_element_type=F32)


def _mod_kernel(c_ref, w_ref, b_ref, o_ref):
    c = c_ref[...]
    s = (c * _sigmoid(c)).astype(BF16)
    o_ref[...] = jnp.dot(s, w_ref[...].astype(BF16), preferred_element_type=F32) + b_ref[...]


def _modulation(cc, w_mod, b_mod):
    rows, d = cc.shape
    n = w_mod.shape[1]
    tn = 1024
    return pl.pallas_call(
        _mod_kernel,
        out_shape=jax.ShapeDtypeStruct((rows, n), F32),
        grid=(n // tn,),
        in_specs=[pl.BlockSpec((rows, d), lambda j: (0, 0)),
                  pl.BlockSpec((d, tn), lambda j: (0, j)),
                  pl.BlockSpec((1, tn), lambda j: (0, j))],
        out_specs=pl.BlockSpec((rows, tn), lambda j: (0, j)),
        compiler_params=pltpu.CompilerParams(dimension_semantics=("arbitrary",)),
        name="mod",
    )(cc, w_mod, b_mod)


def _in_proj_kernel(x_ref, sh_ref, sc_ref, nw_ref, w_ref, pa_ref, pb_ref, pz_ref, pg_ref):
    x = x_ref[0]
    ms = jnp.mean(x * x, axis=-1, keepdims=True)
    h = x * lax.rsqrt(ms + EPS) * nw_ref[...]
    h = (h * (1.0 + sc_ref[0]) + sh_ref[0]).astype(BF16)
    col = 0
    for ref, dt in ((pa_ref, BF16), (pb_ref, BF16), (pz_ref, F32), (pg_ref, F32)):
        width = ref.shape[-1]
        step = min(width, 512)
        for j in range(width // step):
            y = jnp.dot(h, w_ref[:, col:col + step], preferred_element_type=F32)
            ref[0, :, j * step:(j + 1) * step] = y.astype(dt)
            col += step


def _in_proj(x, sh, sc, norm_w, w_cat, tm):
    b, t, d = x.shape
    nc = w_cat.shape[1]
    widths = (4 * MIX_W, 3 * MIX_W, 2 * MIX_W, GATE_PAD)
    assert sum(widths) == nc
    mod_idx = (lambda bi, i: (bi, 0, 0)) if sh.shape[0] > 1 else (lambda bi, i: (0, 0, 0))
    out_shape = [jax.ShapeDtypeStruct((b, t, widths[0]), BF16),
                 jax.ShapeDtypeStruct((b, t, widths[1]), BF16),
                 jax.ShapeDtypeStruct((b, t, widths[2]), F32),
                 jax.ShapeDtypeStruct((b, t, widths[3]), F32)]
    return pl.pallas_call(
        _in_proj_kernel,
        out_shape=out_shape,
        grid=(b, t // tm),
        in_specs=[pl.BlockSpec((1, tm, d), lambda bi, i: (bi, i, 0)),
                  pl.BlockSpec((1, 1, d), mod_idx),
                  pl.BlockSpec((1, 1, d), mod_idx),
                  pl.BlockSpec((1, d), lambda bi, i: (0, 0)),
                  pl.BlockSpec((d, nc), lambda bi, i: (0, 0))],
        out_specs=[pl.BlockSpec((1, tm, w), lambda bi, i: (bi, i, 0)) for w in widths],
        compiler_params=pltpu.CompilerParams(dimension_semantics=("arbitrary", "arbitrary"),
                                             vmem_limit_bytes=V7X_VMEM_LIMIT),
        name="in_proj",
    )(x, sh, sc, norm_w, w_cat)


def _conv3x3(u, w9, width, rows_out, two_d):
    ntok = u.shape[0]
    col = lax.broadcasted_iota(jnp.int32, (ntok, 1), 0) % width
    left = jnp.where(col == 0, 0.0, pltpu.roll(u, 1, 0))
    right = jnp.where(col == width - 1, 0.0, pltpu.roll(u, ntok - 1, 0))
    srcs = (left, u, right)
    acc = None
    for dr in (range(3) if two_d else (1,)):
        lo = dr * width if two_d else 0
        for dc in range(3):
            term = srcs[dc][lo:lo + rows_out * width] * w9[dr * 3 + dc:dr * 3 + dc + 1, :]
            acc = term if acc is None else acc + term
    return acc


def _halo_specs(rows_per_blk, width, n_rows, chan_blk):
    prev = pl.BlockSpec((1, width, chan_blk),
                        lambda bi, i, c: (bi, jnp.maximum(i * rows_per_blk - 1, 0), c))
    nxt = pl.BlockSpec((1, width, chan_blk),
                       lambda bi, i, c: (bi, jnp.minimum((i + 1) * rows_per_blk, n_rows - 1), c))
    return prev, nxt


def _qk_conv_kernel(*refs, width, rows_per_blk, two_d):
    if two_d:
        main_ref, prev_ref, next_ref, w_ref, o_ref = refs
        i = pl.program_id(1)
        last = pl.num_programs(1) - 1
        prev = jnp.where(i > 0, prev_ref[0].astype(F32), 0.0)
        nxt = jnp.where(i < last, next_ref[0].astype(F32), 0.0)
        u = jnp.concatenate([prev, main_ref[0].astype(F32), nxt], axis=0)
    else:
        main_ref, w_ref, o_ref = refs
        u = main_ref[0].astype(F32)
    y = _conv3x3(u, w_ref[...], width, rows_per_blk, two_d)
    y = y * _sigmoid(y)
    scale = jnp.where(pl.program_id(2) == 0, HEAD_DIM ** -0.5, 1.0)
    o_ref[0] = (y * scale).astype(BF16)


def _qk_conv(pa, w9, width, two_d):
    b, t, _ = pa.shape
    cb = MIX_W
    if two_d:
        rows_per_blk = 8
        n_rows = t // width
        grid = (b, n_rows // rows_per_blk, 2)
        prev, nxt = _halo_specs(rows_per_blk, width, n_rows, cb)
        in_specs = [pl.BlockSpec((1, rows_per_blk * width, cb), lambda bi, i, c: (bi, i, c)), prev, nxt]
        args = (pa, pa, pa)
    else:
        rows_per_blk = 1
        assert t == width
        grid = (b, 1, 2)
        in_specs = [pl.BlockSpec((1, width, cb), lambda bi, i, c: (bi, 0, c))]
        args = (pa,)
    in_specs.append(pl.BlockSpec((9, cb), lambda bi, i, c: (0, c)))
    return pl.pallas_call(
        functools.partial(_qk_conv_kernel, width=width, rows_per_blk=rows_per_blk, two_d=two_d),
        out_shape=jax.ShapeDtypeStruct((b, t, 2 * MIX_W), BF16),
        grid=grid,
        in_specs=in_specs,
        out_specs=pl.BlockSpec((1, rows_per_blk * width, cb), lambda bi, i, c: (bi, i, c)),
        compiler_params=pltpu.CompilerParams(dimension_semantics=("arbitrary",) * 3,
                                             vmem_limit_bytes=V7X_VMEM_LIMIT),
        name="qk_conv",
    )(*args, w9)


def _mlstm_kernel(*refs, chunk, need_h):
    ins = refs[:13]
    q_refs, k_refs, v_refs, g_refs = ins[0:2], ins[2:4], ins[4:6], ins[6:8]
    gbias_ref, cum_ref, c0_ref, n0_ref, m0_ref = ins[8:13]
    outs = refs[13:]
    if need_h:
        h_refs, (c_ref, n_ref, m_ref) = outs[0:2], outs[2:5]
    else:
        c_ref, n_ref, m_ref = outs

    @pl.when(pl.program_id(1) == 0)
    def _():
        c_ref[...] = c0_ref[...]
        n_ref[...] = n0_ref[...]
        m_ref[...] = m0_ref[...]

    L = chunk
    t_idx = lax.broadcasted_iota(jnp.int32, (L, L), 0)
    s_idx = lax.broadcasted_iota(jnp.int32, (L, L), 1)
    for d in range(2):
        mask = (s_idx <= t_idx) if d == 0 else (s_idx >= t_idx)
        last = L - 1 if d == 0 else 0
        g = g_refs[d][0] + gbias_ref[...]
        lf = _log_sigmoid(g)
        bc = jnp.dot(cum_ref[d], lf, precision=HIGHEST, preferred_element_type=F32)
        ut = (g - pltpu.roll(bc, GATE_PAD - N_HEADS, 1)).T
        for h in range(N_HEADS):
            idx = d * N_HEADS + h
            ic, fc = 2 * N_HEADS * d + h, 2 * N_HEADS * d + N_HEADS + h
            hs = slice(h * HEAD_DIM, (h + 1) * HEAD_DIM)
            b_col = bc[:, fc:fc + 1]
            i_col = g[:, ic:ic + 1]
            b_last = bc[last:last + 1, fc:fc + 1]
            m_old = m_ref[0, idx:idx + 1, 0:1]
            c_old = c_ref[0, idx]
            n_old = n_ref[0, idx:idx + 1, :]
            qh = q_refs[d][0, :, hs]
            kh = k_refs[d][0, :, hs]
            vh = v_refs[d][0, :, hs]
            if need_h:
                dlog = jnp.where(mask, b_col + ut[ic:ic + 1, :], -jnp.inf)
                inter = b_col + m_old
                mt = jnp.maximum(inter, jnp.max(dlog, axis=-1, keepdims=True))
                w_intra = jnp.exp(dlog - mt)
                w_inter = jnp.exp(inter - mt)
                s = _dot_nt(qh, kh) * w_intra
                num = (jnp.dot(s.astype(BF16), vh, preferred_element_type=F32)
                       + w_inter * jnp.dot(qh, c_old.astype(BF16), preferred_element_type=F32))
                den = (jnp.sum(s, axis=-1, keepdims=True)
                       + w_inter * jnp.sum(qh.astype(F32) * n_old, axis=-1, keepdims=True))
                hh = num / jnp.maximum(jnp.abs(den), jnp.exp(-mt))
                h_refs[d][0, :, hs] = hh.astype(h_refs[d].dtype)
            wlog = b_last - b_col + i_col
            m_new = jnp.maximum(b_last + m_old, jnp.max(wlog, axis=0, keepdims=True))
            decay = jnp.exp(b_last + m_old - m_new)
            kw = kh.astype(F32) * jnp.exp(wlog - m_new)
            c_ref[0, idx] = decay * c_old + _tdot(kw, vh)
            n_ref[0, idx:idx + 1, :] = decay * n_old + jnp.sum(kw, axis=0, keepdims=True)
            m_ref[0, idx:idx + 1, :] = jnp.broadcast_to(m_new, (1, HEAD_DIM))


def _cum_masks(chunk):
    t = np.arange(chunk)
    fwd = (t[None, :] <= t[:, None]).astype(np.float32)
    return jnp.asarray(np.stack([fwd, fwd.T]))


def _mlstm_scan(qk, pa, pg, gate_b, state, need_h):
    b, t, _ = qk.shape
    L = ML_CHUNK
    nc = t // L
    fwd = lambda col: (lambda bi, j: (bi, j, col))
    bwd = lambda col: (lambda bi, j: (bi, nc - 1 - j, col))
    blk = lambda w, im: pl.BlockSpec((1, L, w), im)
    whole = lambda shape: pl.BlockSpec(shape, lambda bi, j: (0,) * len(shape))
    st_c = pl.BlockSpec((1, 2 * N_HEADS, HEAD_DIM, HEAD_DIM), lambda bi, j: (bi, 0, 0, 0))
    st_v = pl.BlockSpec((1, 2 * N_HEADS, HEAD_DIM), lambda bi, j: (bi, 0, 0))
    in_specs = [blk(MIX_W, fwd(0)), blk(MIX_W, bwd(0)),
                blk(MIX_W, fwd(1)), blk(MIX_W, bwd(1)),
                blk(MIX_W, fwd(2)), blk(MIX_W, bwd(2)),
                blk(GATE_PAD, fwd(0)), blk(GATE_PAD, bwd(0)),
                whole((1, GATE_PAD)), whole((2, L, L)), st_c, st_v, st_v]
    state_shapes = [jax.ShapeDtypeStruct((b, 2 * N_HEADS, HEAD_DIM, HEAD_DIM), F32),
                    jax.ShapeDtypeStruct((b, 2 * N_HEADS, HEAD_DIM), F32),
                    jax.ShapeDtypeStruct((b, 2 * N_HEADS, HEAD_DIM), F32)]
    out_shape, out_specs = state_shapes, [st_c, st_v, st_v]
    if need_h:
        out_shape = [jax.ShapeDtypeStruct((b, t, MIX_W), BF16)] * 2 + out_shape
        out_specs = [blk(MIX_W, fwd(0)), blk(MIX_W, bwd(0))] + out_specs
    res = pl.pallas_call(
        functools.partial(_mlstm_kernel, chunk=L, need_h=need_h),
        out_shape=out_shape,
        grid=(b, nc),
        in_specs=in_specs,
        out_specs=out_specs,
        compiler_params=pltpu.CompilerParams(dimension_semantics=("arbitrary", "arbitrary"),
                                             vmem_limit_bytes=V7X_VMEM_LIMIT),
        name="mlstm_lat" if need_h else "mlstm_ctx",
    )(qk, qk, qk, qk, pa, pa, pg, pg, gate_b, _cum_masks(L), *state)
    return (res[0], res[1], tuple(res[2:])) if need_h else (None, None, tuple(res))


def _hgrn_sum_masks(chunk):
    L = chunk
    t = np.arange(L)[:, None]
    r = np.arange(L)[None, :]
    n_lev = int(np.log2(L))
    fwd = [(r <= t)]
    for lev in range(n_lev):
        w = 1 << lev
        ref = (t // (2 * w)) * 2 * w + w - 1
        is_q = (t // w) % 2 == 1
        fwd.append(np.where(is_q, (r > ref) & (r <= t), (r > t) & (r <= ref)))
    fwd = np.stack([m.astype(np.float32) for m in fwd])
    bwd = fwd[:, ::-1, ::-1]
    sums = np.stack([fwd.reshape(-1, L), bwd.reshape(-1, L)])
    pair = np.stack([((t // (2 << lev)) == (r // (2 << lev))) for lev in range(n_lev)]).astype(np.float32)
    return jnp.asarray(sums, BF16), jnp.asarray(pair)


def _exact_mask_dot(mask, x):
    hi = x.astype(BF16)
    r1 = x - hi.astype(F32)
    mid = r1.astype(BF16)
    lo = (r1 - mid.astype(F32)).astype(BF16)
    return (jnp.dot(mask, hi, preferred_element_type=F32) + jnp.dot(mask, mid, preferred_element_type=F32)
            + jnp.dot(mask, lo, preferred_element_type=F32))


def _hgrn_kernel(*refs, chunk, block, need_o):
    q_refs, v_refs, z_refs = refs[0:2], refs[2:4], refs[4:6]
    lbl_ref, sums_ref, pair_ref, s0_ref = refs[6:10]
    outs = refs[10:]
    if need_o:
        o_refs, s_ref = outs[0:2], outs[2]
    else:
        (s_ref,) = outs

    @pl.when(pl.program_id(1) == 0)
    def _():
        s_ref[...] = s0_ref[...]

    L = chunk
    n_lev = L.bit_length() - 1
    n_chunks = block // L
    tok = lax.broadcasted_iota(jnp.int32, (L, 1), 0)

    one_minus_lb = []
    for d in range(2):
        lg = lbl_ref[d]
        e = jnp.exp(lg - jnp.max(lg, axis=0, keepdims=True))
        one_minus_lb.append(1.0 - e[0:1, :] / jnp.sum(e, axis=0, keepdims=True))

    def chunk_step(d, cc):
        last = L - 1 if d == 0 else 0
        rows = pl.ds(pl.multiple_of(cc * L, L), L)
        z = z_refs[d][0, rows, :]
        kk = one_minus_lb[d] / (1.0 + jnp.exp(z))
        lf = jnp.log1p(-kk)
        sums = _exact_mask_dot(sums_ref[d], lf)
        v = v_refs[d][0, rows, :]
        vf = v.astype(F32)
        if need_o:
            qa = q_refs[d][0, rows, :].astype(F32)
            qs = qa * _sigmoid(qa) * (HEAD_DIM ** -0.5)
        for h in range(N_HEADS):
            idx = d * N_HEADS + h
            hs = slice(h * HEAD_DIM, (h + 1) * HEAD_DIM)
            b = sums[0:L, hs]
            b_last = b[last:last + 1, :]
            k = kk[:, hs]
            st_old = s_ref[0, idx]
            if need_o:
                q = qs[:, hs]
                a = jnp.zeros((L, L), F32)
                for lev in range(n_lev):
                    w = 1 << lev
                    is_q = ((tok // w) % 2) == (1 if d == 0 else 0)
                    ex = jnp.exp(sums[(1 + lev) * L:(2 + lev) * L, hs])
                    ql = jnp.where(is_q, q * ex, 0.0).astype(BF16)
                    kl = jnp.where(is_q, 0.0, k * ex).astype(BF16)
                    a = a + pair_ref[lev] * _dot_nt(ql, kl)
                o = (jnp.dot(a.astype(BF16), v[:, hs], preferred_element_type=F32)
                     + jnp.sum(q * k, axis=-1, keepdims=True) * vf[:, hs]
                     + _dot_nt((q * jnp.exp(b)).astype(BF16), st_old.astype(BF16)))
                o_refs[d][0, rows, hs] = o.astype(o_refs[d].dtype)
            kd = k * jnp.exp(b_last - b)
            s_ref[0, idx] = st_old * jnp.exp(b_last) + _tdot(vf[:, hs], kd)

    def body(c, carry):
        chunk_step(0, c)
        chunk_step(1, n_chunks - 1 - c)
        return carry

    lax.fori_loop(0, n_chunks, body, 0, unroll=2)


def _hgrn_scan(pb, pz, lb_logits, state, need_o):
    b, t, _ = pb.shape
    L, blk_t = HG_CHUNK, min(HG_BLOCK, t)
    nb = t // blk_t
    fwd = lambda col: (lambda bi, j: (bi, j, col))
    bwd = lambda col: (lambda bi, j: (bi, nb - 1 - j, col))
    blk = lambda im: pl.BlockSpec((1, blk_t, MIX_W), im)
    whole = lambda shape: pl.BlockSpec(shape, lambda bi, j: (0,) * len(shape))
    sums, pair = _hgrn_sum_masks(L)
    st = pl.BlockSpec((1, 2 * N_HEADS, HEAD_DIM, HEAD_DIM), lambda bi, j: (bi, 0, 0, 0))
    in_specs = [blk(fwd(0)), blk(bwd(0)),
                blk(fwd(1)), blk(bwd(1)),
                blk(fwd(0)), blk(bwd(1)),
                whole(lb_logits.shape), whole(sums.shape), whole(pair.shape), st]
    st_shape = jax.ShapeDtypeStruct((b, 2 * N_HEADS, HEAD_DIM, HEAD_DIM), F32)
    out_shape, out_specs = [st_shape], [st]
    if need_o:
        out_shape = [jax.ShapeDtypeStruct((b, t, MIX_W), BF16)] * 2 + out_shape
        out_specs = [blk(fwd(0)), blk(bwd(0))] + out_specs
    res = pl.pallas_call(
        functools.partial(_hgrn_kernel, chunk=L, block=blk_t, need_o=need_o),
        out_shape=out_shape,
        grid=(b, nb),
        in_specs=in_specs,
        out_specs=out_specs,
        compiler_params=pltpu.CompilerParams(dimension_semantics=("arbitrary", "arbitrary"),
                                             vmem_limit_bytes=V7X_VMEM_LIMIT),
        name="hgrn_lat" if need_o else "hgrn_ctx",
    )(pb, pb, pb, pb, pz, pz, lb_logits, sums, pair, state)
    return (res[0], res[1], res[2]) if need_o else (None, None, res[0])


def _head_norm(hsum, w):
    parts = []
    for h in range(N_HEADS):
        blk = hsum[:, h * HEAD_DIM:(h + 1) * HEAD_DIM]
        parts.append(blk * lax.rsqrt(jnp.mean(blk * blk, axis=-1, keepdims=True) + EPS))
    return jnp.concatenate(parts, axis=-1) * w


def _out_proj_kernel(x_ref, mf_ref, mb_ref, o_ref, hf_ref, hb_ref, g_ref, mw_ref, hw_ref, wout_ref,
                     g1_ref, sh2_ref, sc2_ref, n2w_ref, x1_ref, h2_ref):
    ml = _head_norm(mf_ref[0].astype(F32) + mb_ref[0].astype(F32), mw_ref[...])
    ml = ml * _sigmoid(o_ref[0].astype(F32))
    hg = _head_norm(hf_ref[0].astype(F32) + hb_ref[0].astype(F32), hw_ref[...])
    gg = g_ref[0].astype(F32)
    hg = hg * (gg * _sigmoid(gg))
    mix = (jnp.dot(ml.astype(BF16), wout_ref[0:MIX_W, :], preferred_element_type=F32)
           + jnp.dot(hg.astype(BF16), wout_ref[MIX_W:2 * MIX_W, :], preferred_element_type=F32))
    x1 = x_ref[0] + g1_ref[0] * mix
    x1_ref[0] = x1
    y = x1 * lax.rsqrt(jnp.mean(x1 * x1, axis=-1, keepdims=True) + EPS) * n2w_ref[...]
    h2_ref[0] = (y * (1.0 + sc2_ref[0]) + sh2_ref[0]).astype(BF16)


def _out_proj(x, mf, mb, pa, hf, hb, pb, ml_norm_w, hg_norm_w, w_out, g1, sh2, sc2, norm2_w, tm):
    b, t, d = x.shape
    tok = lambda w, col: pl.BlockSpec((1, tm, w), lambda bi, i: (bi, i, col))
    per_b = pl.BlockSpec((1, 1, d), lambda bi, i: (bi, 0, 0))
    whole = lambda shape: pl.BlockSpec(shape, lambda bi, i: (0,) * len(shape))
    return pl.pallas_call(
        _out_proj_kernel,
        out_shape=[jax.ShapeDtypeStruct((b, t, d), F32), jax.ShapeDtypeStruct((b, t, d), BF16)],
        grid=(b, t // tm),
        in_specs=[tok(d, 0), tok(MIX_W, 0), tok(MIX_W, 0), tok(MIX_W, 3),
                  tok(MIX_W, 0), tok(MIX_W, 0), tok(MIX_W, 2),
                  whole((1, MIX_W)), whole((1, MIX_W)), whole(w_out.shape),
                  per_b, per_b, per_b, whole((1, d))],
        out_specs=[tok(d, 0), tok(d, 0)],
        compiler_params=pltpu.CompilerParams(dimension_semantics=("arbitrary", "arbitrary"),
                                             vmem_limit_bytes=V7X_VMEM_LIMIT),
        name="out_proj",
    )(x, mf, mb, pa, hf, hb, pb, ml_norm_w, hg_norm_w, w_out, g1, sh2, sc2, norm2_w)


def _ffn_kernel(h_ref, hp_ref, hn_ref, x1_ref, g2_ref, wup_ref, cw_ref, wdn_ref, fw_ref, o_ref,
                *, width, rows_per_blk, chan_chunk):
    i = pl.program_id(1)
    last = pl.num_programs(1) - 1
    hp = jnp.where(i > 0, hp_ref[0], jnp.zeros_like(hp_ref[0]))
    hn = jnp.where(i < last, hn_ref[0], jnp.zeros_like(hn_ref[0]))
    hin = jnp.concatenate([hp, h_ref[0], hn], axis=0)
    d_ff = wdn_ref.shape[0]
    acc = None
    for c in range(d_ff // chan_chunk):
        halves = []
        for base in (0, d_ff):
            cols = slice(base + c * chan_chunk, base + (c + 1) * chan_chunk)
            u = jnp.dot(hin, wup_ref[:, cols], preferred_element_type=F32)
            halves.append(_conv3x3(u, cw_ref[:, cols], width, rows_per_blk, True))
        a, bgate = halves
        act = (a * _sigmoid(a) * bgate).astype(BF16)
        part = jnp.dot(act, wdn_ref[c * chan_chunk:(c + 1) * chan_chunk, :], preferred_element_type=F32)
        acc = part if acc is None else acc + part
    x2 = x1_ref[0] + g2_ref[0] * acc
    o_ref[0] = x2 * lax.rsqrt(jnp.mean(x2 * x2, axis=-1, keepdims=True) + EPS) * fw_ref[...]


def _conv_ffn(h2, x1, g2, w_up, conv_w9, w_down, final_w):
    b, t, d = x1.shape
    width, rows_per_blk = GRID_W, FFN_ROWS
    n_rows = t // width
    tm = rows_per_blk * width
    prev = pl.BlockSpec((1, width, d), lambda bi, i: (bi, jnp.maximum(i * rows_per_blk - 1, 0), 0))
    nxt = pl.BlockSpec((1, width, d),
                       lambda bi, i: (bi, jnp.minimum((i + 1) * rows_per_blk, n_rows - 1), 0))
    tok = pl.BlockSpec((1, tm, d), lambda bi, i: (bi, i, 0))
    whole = lambda shape: pl.BlockSpec(shape, lambda bi, i: (0,) * len(shape),
                                       pipeline_mode=pl.Buffered(1))
    return pl.pallas_call(
        functools.partial(_ffn_kernel, width=width, rows_per_blk=rows_per_blk, chan_chunk=FFN_CW),
        out_shape=jax.ShapeDtypeStruct((b, t, d), F32),
        grid=(b, n_rows // rows_per_blk),
        in_specs=[tok, prev, nxt, tok, pl.BlockSpec((1, 1, d), lambda bi, i: (bi, 0, 0)),
                  whole(w_up.shape), whole(conv_w9.shape), whole(w_down.shape), whole((1, d))],
        out_specs=tok,
        compiler_params=pltpu.CompilerParams(dimension_semantics=("arbitrary", "arbitrary"),
                                             vmem_limit_bytes=V7X_VMEM_LIMIT),
        name="conv_ffn",
    )(h2, h2, h2, x1, g2, w_up, conv_w9, w_down, final_w)


def _reorder_w_in(w_in):
    d = w_in.shape[0]
    ml, hg = 4 * MIX_W, 4 * MIX_W + 4 * N_HEADS
    gates = jnp.pad(w_in[:, ml:hg], ((0, 0), (0, GATE_PAD - 4 * N_HEADS)))
    cols = [w_in[:, :ml],
            w_in[:, hg:hg + MIX_W], w_in[:, hg + 3 * MIX_W:hg + 5 * MIX_W],
            w_in[:, hg + MIX_W:hg + 3 * MIX_W], gates]
    return jnp.concatenate(cols, axis=1).astype(BF16)


def kernel(x, c, ctx, c_ctx, w_mod, b_mod, norm1_w, w_in, mlstm_gate_b, mlstm_conv_w, mlstm_norm_w,
           hgrn_lb_logits, hgrn_norm_w, w_out, norm2_w, w_up, ffn_conv_w, w_down, final_norm_w):
    b, t, d = x.shape
    assert w_mod.shape[0] == 1 and hgrn_lb_logits.shape[1] == 2, "single-layer block"
    assert d == 2 * MIX_W and t % (GRID_W * FFN_ROWS) == 0

    rows = -(-(b + 1) // 8) * 8
    cc = jnp.concatenate([c, c_ctx[None, :], jnp.zeros((rows - b - 1, d), F32)], axis=0)
    mod = _modulation(cc, w_mod[0], b_mod)
    mod_l = mod[:b].reshape(b, N_MOD, 1, d)
    sh1, sc1, g1, sh2, sc2, g2 = (mod_l[:, j] for j in range(N_MOD))
    mod_c = mod[b:b + 1].reshape(1, N_MOD, 1, d)
    sh1c, sc1c = mod_c[:, 0], mod_c[:, 1]

    w_cat = _reorder_w_in(w_in[0])
    pa, pb, pz, pg = _in_proj(x, sh1, sc1, norm1_w, w_cat, TOK_TILE)
    pa_c, pb_c, pz_c, pg_c = _in_proj(ctx, sh1c, sc1c, norm1_w, w_cat, ctx.shape[1])

    conv9 = mlstm_conv_w[0].reshape(9, 2 * MIX_W)
    qk = _qk_conv(pa, conv9, GRID_W, True)
    qk_c = _qk_conv(pa_c, conv9, ctx.shape[1], False)

    gate_b = jnp.pad(mlstm_gate_b, ((0, 0), (0, GATE_PAD - mlstm_gate_b.shape[1])))
    zero_state = (jnp.zeros((b, 2 * N_HEADS, HEAD_DIM, HEAD_DIM), F32),
                  jnp.zeros((b, 2 * N_HEADS, HEAD_DIM), F32),
                  jnp.zeros((b, 2 * N_HEADS, HEAD_DIM), F32))
    _, _, ml_state = _mlstm_scan(qk_c, pa_c, pg_c, gate_b, zero_state, False)
    mf, mb, _ = _mlstm_scan(qk, pa, pg, gate_b, ml_state, True)

    _, _, hg_state = _hgrn_scan(pb_c, pz_c, hgrn_lb_logits, zero_state[0], False)
    hf, hb, _ = _hgrn_scan(pb, pz, hgrn_lb_logits, hg_state, True)

    x1, h2 = _out_proj(x, mf, mb, pa, hf, hb, pb, mlstm_norm_w, hgrn_norm_w, w_out[0].astype(BF16),
                       g1, sh2, sc2, norm2_w, TOK_TILE)
    return _conv_ffn(h2, x1, g2, w_up[0].astype(BF16), ffn_conv_w[0].reshape(9, -1),
                     w_down[0].astype(BF16), final_norm_w[None, :])
```

```python
import functools

import numpy as np
import jax
import jax.numpy as jnp
from jax import lax
from jax.experimental import pallas as pl
from jax.experimental.pallas import tpu as pltpu

F32 = jnp.float32
BF16 = jnp.bfloat16
HIGHEST = lax.Precision.HIGHEST

EPS = 1e-6
GRID_W = 64
HEAD_DIM = 128
N_HEADS = 4
MIX_W = N_HEADS * HEAD_DIM
N_MOD = 6
GATE_PAD = 128
GATE_ROWS = 32
V7X_VMEM_LIMIT = 56 * 1024 * 1024

ML_CHUNK = 256
HG_CHUNK = 32
HG_BLOCK = 256
TOK_TILE = 512
FFN_ROWS = 8
FFN_CW = 256


def _sigmoid(x):
    return 1.0 / (1.0 + jnp.exp(-x))


def _log_sigmoid(x):
    return jnp.minimum(x, 0.0) - jnp.log1p(jnp.exp(-jnp.abs(x)))


def _tdot(a, b):
    return jnp.dot(a.T.astype(BF16), b.astype(BF16), preferred_element_type=F32)


def _dot_nt(a, b):
    return lax.dot_general(a, b, (((1,), (1,)), ((), ())), preferred_element_type=F32)


def _mod_kernel(c_ref, w_ref, b_ref, o_ref):
    c = c_ref[...]
    s = (c * _sigmoid(c)).astype(BF16)
    o_ref[...] = jnp.dot(s, w_ref[...].astype(BF16), preferred_element_type=F32) + b_ref[...]


def _modulation(cc, w_mod, b_mod):
    rows, d = cc.shape
    n = w_mod.shape[1]
    tn = 1024
    return pl.pallas_call(
        _mod_kernel,
        out_shape=jax.ShapeDtypeStruct((rows, n), F32),
        grid=(n // tn,),
        in_specs=[pl.BlockSpec((rows, d), lambda j: (0, 0)),
                  pl.BlockSpec((d, tn), lambda j: (0, j)),
                  pl.BlockSpec((1, tn), lambda j: (0, j))],
        out_specs=pl.BlockSpec((rows, tn), lambda j: (0, j)),
        compiler_params=pltpu.CompilerParams(dimension_semantics=("arbitrary",)),
        name="mod",
    )(cc, w_mod, b_mod)


def _in_proj_kernel(x_ref, sh_ref, sc_ref, nw_ref, w_ref, gb_ref, pa_ref, pb_ref, pz_ref, pg_ref):
    x = x_ref[0]
    ms = jnp.mean(x * x, axis=-1, keepdims=True)
    h = x * lax.rsqrt(ms + EPS) * nw_ref[...]
    h = (h * (1.0 + sc_ref[0]) + sh_ref[0]).astype(BF16)
    col = 0
    for ref, dt in ((pa_ref, BF16), (pb_ref, BF16), (pz_ref, F32)):
        width = ref.shape[-1]
        step = min(width, 512)
        for j in range(width // step):
            y = jnp.dot(h, w_ref[:, col:col + step], preferred_element_type=F32)
            ref[0, :, j * step:(j + 1) * step] = y.astype(dt)
            col += step
    g = jnp.dot(h, w_ref[:, col:col + GATE_PAD], preferred_element_type=F32) + gb_ref[...]
    pg_ref[0] = g.T[0:GATE_ROWS, :]


def _in_proj(x, sh, sc, norm_w, w_cat, gate_b, tm):
    b, t, d = x.shape
    nc = w_cat.shape[1]
    widths = (4 * MIX_W, 3 * MIX_W, 2 * MIX_W)
    assert sum(widths) + GATE_PAD == nc
    mod_idx = (lambda bi, i: (bi, 0, 0)) if sh.shape[0] > 1 else (lambda bi, i: (0, 0, 0))
    out_shape = [jax.ShapeDtypeStruct((b, t, widths[0]), BF16),
                 jax.ShapeDtypeStruct((b, t, widths[1]), BF16),
                 jax.ShapeDtypeStruct((b, t, widths[2]), F32),
                 jax.ShapeDtypeStruct((b, GATE_ROWS, t), F32)]
    return pl.pallas_call(
        _in_proj_kernel,
        out_shape=out_shape,
        grid=(b, t // tm),
        in_specs=[pl.BlockSpec((1, tm, d), lambda bi, i: (bi, i, 0)),
                  pl.BlockSpec((1, 1, d), mod_idx),
                  pl.BlockSpec((1, 1, d), mod_idx),
                  pl.BlockSpec((1, d), lambda bi, i: (0, 0)),
                  pl.BlockSpec((d, nc), lambda bi, i: (0, 0)),
                  pl.BlockSpec((1, GATE_PAD), lambda bi, i: (0, 0))],
        out_specs=[pl.BlockSpec((1, tm, w), lambda bi, i: (bi, i, 0)) for w in widths]
        + [pl.BlockSpec((1, GATE_ROWS, tm), lambda bi, i: (bi, 0, i))],
        compiler_params=pltpu.CompilerParams(dimension_semantics=("arbitrary", "arbitrary"),
                                             vmem_limit_bytes=V7X_VMEM_LIMIT),
        name="in_proj",
    )(x, sh, sc, norm_w, w_cat, gate_b)


def _conv3x3(u, w9, width, rows_out, two_d):
    ntok = u.shape[0]
    col = lax.broadcasted_iota(jnp.int32, (ntok, 1), 0) % width
    left = jnp.where(col == 0, 0.0, pltpu.roll(u, 1, 0))
    right = jnp.where(col == width - 1, 0.0, pltpu.roll(u, ntok - 1, 0))
    srcs = (left, u, right)
    acc = None
    for dr in (range(3) if two_d else (1,)):
        lo = dr * width if two_d else 0
        for dc in range(3):
            term = srcs[dc][lo:lo + rows_out * width] * w9[dr * 3 + dc:dr * 3 + dc + 1, :]
            acc = term if acc is None else acc + term
    return acc


CONV_PAD = 8


def _stage_init(scr, n_in):
    zeros = jnp.zeros((CONV_PAD, scr.shape[1]), F32)
    scr[0:CONV_PAD, :] = zeros
    scr[CONV_PAD + n_in:2 * CONV_PAD + n_in, :] = zeros


def _stage_store(scr, u):
    scr[CONV_PAD:CONV_PAD + u.shape[0], :] = u


def _conv3x3_staged(scr, w9, width, rows_out):
    n_out = rows_out * width
    col = lax.broadcasted_iota(jnp.int32, (n_out, 1), 0) % width

    def col_tap(dc):
        acc = None
        for dr in range(3):
            src = scr[CONV_PAD + dr * width + dc - 1:CONV_PAD + dr * width + dc - 1 + n_out, :]
            term = src * w9[dr * 3 + dc:dr * 3 + dc + 1, :]
            acc = term if acc is None else acc + term
        return acc

    return (col_tap(1) + jnp.where(col == 0, 0.0, col_tap(0))
            + jnp.where(col == width - 1, 0.0, col_tap(2)))


def _halo_specs(rows_per_blk, width, n_rows, chan_blk):
    prev = pl.BlockSpec((1, width, chan_blk),
                        lambda bi, i, c: (bi, jnp.maximum(i * rows_per_blk - 1, 0), c))
    nxt = pl.BlockSpec((1, width, chan_blk),
                       lambda bi, i, c: (bi, jnp.minimum((i + 1) * rows_per_blk, n_rows - 1), c))
    return prev, nxt


def _qk_conv_kernel(*refs, width, rows_per_blk, two_d):
    if two_d:
        main_ref, prev_ref, next_ref, w_ref, o_ref = refs
        i = pl.program_id(1)
        last = pl.num_programs(1) - 1
        prev = jnp.where(i > 0, prev_ref[0].astype(F32), 0.0)
        nxt = jnp.where(i < last, next_ref[0].astype(F32), 0.0)
        u = jnp.concatenate([prev, main_ref[0].astype(F32), nxt], axis=0)
    else:
        main_ref, w_ref, o_ref = refs
        u = main_ref[0].astype(F32)
    y = _conv3x3(u, w_ref[...], width, rows_per_blk, two_d)
    y = y * _sigmoid(y)
    scale = jnp.where(pl.program_id(2) == 0, HEAD_DIM ** -0.5, 1.0)
    o_ref[0] = (y * scale).astype(BF16)


def _qk_conv(pa, w9, width, two_d):
    b, t, _ = pa.shape
    cb = MIX_W
    if two_d:
        rows_per_blk = 8
        n_rows = t // width
        grid = (b, n_rows // rows_per_blk, 2)
        prev, nxt = _halo_specs(rows_per_blk, width, n_rows, cb)
        in_specs = [pl.BlockSpec((1, rows_per_blk * width, cb), lambda bi, i, c: (bi, i, c)), prev, nxt]
        args = (pa, pa, pa)
    else:
        rows_per_blk = 1
        assert t == width
        grid = (b, 1, 2)
        in_specs = [pl.BlockSpec((1, width, cb), lambda bi, i, c: (bi, 0, c))]
        args = (pa,)
    in_specs.append(pl.BlockSpec((9, cb), lambda bi, i, c: (0, c)))
    return pl.pallas_call(
        functools.partial(_qk_conv_kernel, width=width, rows_per_blk=rows_per_blk, two_d=two_d),
        out_shape=jax.ShapeDtypeStruct((b, t, 2 * MIX_W), BF16),
        grid=grid,
        in_specs=in_specs,
        out_specs=pl.BlockSpec((1, rows_per_blk * width, cb), lambda bi, i, c: (bi, i, c)),
        compiler_params=pltpu.CompilerParams(dimension_semantics=("arbitrary",) * 3,
                                             vmem_limit_bytes=V7X_VMEM_LIMIT),
        name="qk_conv",
    )(*args, w9)


def _mlstm_kernel(*refs, chunk, need_h):
    q_refs, k_refs, v_refs, g_refs = refs[0:2], refs[2:4], refs[4:6], refs[6:8]
    cum_ref, c0_ref, m0_ref = refs[8:11]
    outs = refs[11:]
    if need_h:
        h_refs, (c_ref, m_ref) = outs[0:2], outs[2:4]
    else:
        c_ref, m_ref = outs

    @pl.when(pl.program_id(1) == 0)
    def _():
        c_ref[...] = c0_ref[...]
        m_ref[...] = m0_ref[...]

    L = chunk
    t_idx = lax.broadcasted_iota(jnp.int32, (L, L), 0)
    s_idx = lax.broadcasted_iota(jnp.int32, (L, L), 1)
    ones_col = jnp.where(lax.broadcasted_iota(jnp.int32, (L, HEAD_DIM), 1) == 0, 1.0, 0.0).astype(BF16)
    for d in range(2):
        mask = (s_idx <= t_idx) if d == 0 else (s_idx >= t_idx)
        last = L - 1 if d == 0 else 0
        g = g_refs[d][0]
        gi = g[16 * d:16 * d + 8]
        lf = _log_sigmoid(g[16 * d + 8:16 * d + 16])
        b = _exact_dot_mask(lf, cum_ref[d])
        u = gi - b
        cm = _scan_max(u, reverse=(d == 1))
        b_last = b[:, last:last + 1]
        cm_last = cm[:, last:last + 1]
        m_old = m_ref[0, d][:, 0:1]
        c_stab = jnp.maximum(cm, m_old)
        a_inter = jnp.exp(m_old - c_stab)
        e_neg_mt = jnp.exp(-(b + c_stab))
        ws = jnp.exp(u - cm_last)
        m_new = jnp.maximum(b_last + m_old, b_last + cm_last)
        decay = jnp.exp(b_last + m_old - m_new)
        gain = jnp.exp(b_last + cm_last - m_new)
        m_ref[0, d] = jnp.broadcast_to(m_new, (8, HEAD_DIM))
        cols = jnp.concatenate([c_stab, a_inter, e_neg_mt, ws, jnp.zeros((HEAD_DIM - 32, L), F32)],
                               axis=0).T
        for h in range(N_HEADS):
            hs = slice(h * HEAD_DIM, (h + 1) * HEAD_DIM)
            col = lambda j: cols[:, 8 * j + h:8 * j + h + 1]
            qh = q_refs[d][0, :, hs]
            kh = k_refs[d][0, :, hs]
            v_ext = jnp.concatenate([v_refs[d][0, :, hs], ones_col], axis=1)
            cn_old = c_ref[0, d, h]
            if need_h:
                w = jnp.where(mask, jnp.exp(u[h:h + 1, :] - col(0)), 0.0)
                s = (_dot_nt(qh, kh) * w).astype(BF16)
                intra = jnp.dot(s, v_ext, preferred_element_type=F32)
                inter = jnp.dot(qh, cn_old.astype(BF16), preferred_element_type=F32)
                comb = intra + col(1) * inter
                den = comb[:, HEAD_DIM:HEAD_DIM + 1]
                hh = comb[:, 0:HEAD_DIM] / jnp.maximum(jnp.abs(den), col(2))
                h_refs[d][0, :, hs] = hh.astype(h_refs[d].dtype)
            kw = kh.astype(F32) * col(3)
            c_ref[0, d, h] = decay[h:h + 1, :] * cn_old + gain[h:h + 1, :] * _tdot(kw, v_ext)


def _exact_dot_mask(x, mask):
    hi = x.astype(BF16)
    r1 = x - hi.astype(F32)
    mid = r1.astype(BF16)
    lo = (r1 - mid.astype(F32)).astype(BF16)
    return (jnp.dot(hi, mask, preferred_element_type=F32) + jnp.dot(mid, mask, preferred_element_type=F32)
            + jnp.dot(lo, mask, preferred_element_type=F32))


def _scan_max(x, reverse):
    n = x.shape[-1]
    lane = lax.broadcasted_iota(jnp.int32, x.shape, 1)
    sh = 1
    while sh < n:
        if reverse:
            x = jnp.where(lane < n - sh, jnp.maximum(x, pltpu.roll(x, n - sh, 1)), x)
        else:
            x = jnp.where(lane >= sh, jnp.maximum(x, pltpu.roll(x, sh, 1)), x)
        sh *= 2
    return x


def _cum_masks(chunk):
    t = np.arange(chunk)
    fwd = (t[:, None] <= t[None, :]).astype(np.float32)
    return jnp.asarray(np.stack([fwd, fwd.T]), BF16)


def _mlstm_scan(qk, pa, pg, state, need_h):
    b, t, _ = qk.shape
    L = ML_CHUNK
    nc = t // L
    fwd = lambda col: (lambda bi, j: (bi, j, col))
    bwd = lambda col: (lambda bi, j: (bi, nc - 1 - j, col))
    blk = lambda im: pl.BlockSpec((1, L, MIX_W), im)
    gate = lambda im: pl.BlockSpec((1, GATE_ROWS, L), im)
    st_c = pl.BlockSpec((1, 2, N_HEADS, HEAD_DIM, 2 * HEAD_DIM), lambda bi, j: (bi, 0, 0, 0, 0))
    st_m = pl.BlockSpec((1, 2, 8, HEAD_DIM), lambda bi, j: (bi, 0, 0, 0))
    in_specs = [blk(fwd(0)), blk(bwd(0)),
                blk(fwd(1)), blk(bwd(1)),
                blk(fwd(2)), blk(bwd(2)),
                gate(lambda bi, j: (bi, 0, j)), gate(lambda bi, j: (bi, 0, nc - 1 - j)),
                pl.BlockSpec((2, L, L), lambda bi, j: (0, 0, 0)), st_c, st_m]
    out_shape = [jax.ShapeDtypeStruct(state[0].shape, F32), jax.ShapeDtypeStruct(state[1].shape, F32)]
    out_specs = [st_c, st_m]
    if need_h:
        out_shape = [jax.ShapeDtypeStruct((b, t, MIX_W), BF16)] * 2 + out_shape
        out_specs = [blk(fwd(0)), blk(bwd(0))] + out_specs
    res = pl.pallas_call(
        functools.partial(_mlstm_kernel, chunk=L, need_h=need_h),
        out_shape=out_shape,
        grid=(b, nc),
        in_specs=in_specs,
        out_specs=out_specs,
        compiler_params=pltpu.CompilerParams(dimension_semantics=("arbitrary", "arbitrary"),
                                             vmem_limit_bytes=V7X_VMEM_LIMIT),
        name="mlstm_lat" if need_h else "mlstm_ctx",
    )(qk, qk, qk, qk, pa, pa, pg, pg, _cum_masks(L), *state)
    return (res[0], res[1], tuple(res[2:])) if need_h else (None, None, tuple(res))


def _hgrn_sum_masks(chunk):
    L = chunk
    t = np.arange(L)[:, None]
    r = np.arange(L)[None, :]
    n_lev = int(np.log2(L))
    fwd = [(r <= t)]
    for lev in range(n_lev):
        w = 1 << lev
        ref = (t // (2 * w)) * 2 * w + w - 1
        is_q = (t // w) % 2 == 1
        fwd.append(np.where(is_q, (r > ref) & (r <= t), (r > t) & (r <= ref)))
    fwd = np.stack([m.astype(np.float32) for m in fwd])
    bwd = fwd[:, ::-1, ::-1]
    sums = np.stack([fwd.reshape(-1, L), bwd.reshape(-1, L)])
    pair = np.stack([((t // (2 << lev)) == (r // (2 << lev))) for lev in range(n_lev)]).astype(np.float32)
    return jnp.asarray(sums, BF16), jnp.asarray(pair)


def _exact_mask_dot(mask, x):
    hi = x.astype(BF16)
    r1 = x - hi.astype(F32)
    mid = r1.astype(BF16)
    lo = (r1 - mid.astype(F32)).astype(BF16)
    return (jnp.dot(mask, hi, preferred_element_type=F32) + jnp.dot(mask, mid, preferred_element_type=F32)
            + jnp.dot(mask, lo, preferred_element_type=F32))


def _hgrn_kernel(*refs, chunk, block, need_o):
    q_refs, v_refs, z_refs = refs[0:2], refs[2:4], refs[4:6]
    lbl_ref, sums_ref, pair_ref, s0_ref = refs[6:10]
    outs = refs[10:]
    if need_o:
        o_refs, s_ref = outs[0:2], outs[2]
    else:
        (s_ref,) = outs

    @pl.when(pl.program_id(1) == 0)
    def _():
        s_ref[...] = s0_ref[...]

    L = chunk
    n_lev = L.bit_length() - 1
    n_chunks = block // L
    tok = lax.broadcasted_iota(jnp.int32, (L, 1), 0)

    one_minus_lb = []
    for d in range(2):
        lg = lbl_ref[d]
        e = jnp.exp(lg - jnp.max(lg, axis=0, keepdims=True))
        one_minus_lb.append(1.0 - e[0:1, :] / jnp.sum(e, axis=0, keepdims=True))

    def chunk_step(d, cc):
        last = L - 1 if d == 0 else 0
        rows = pl.ds(pl.multiple_of(cc * L, L), L)
        z = z_refs[d][0, rows, :]
        kk = one_minus_lb[d] / (1.0 + jnp.exp(z))
        lf = jnp.log1p(-kk)
        sums = _exact_mask_dot(sums_ref[d], lf)
        v = v_refs[d][0, rows, :]
        vf = v.astype(F32)
        if need_o:
            qa = q_refs[d][0, rows, :].astype(F32)
            qs = qa * _sigmoid(qa) * (HEAD_DIM ** -0.5)
        for h in range(N_HEADS):
            idx = d * N_HEADS + h
            hs = slice(h * HEAD_DIM, (h + 1) * HEAD_DIM)
            b = sums[0:L, hs]
            b_last = b[last:last + 1, :]
            k = kk[:, hs]
            st_old = s_ref[0, idx]
            if need_o:
                q = qs[:, hs]
                a = jnp.zeros((L, L), F32)
                for lev in range(n_lev):
                    w = 1 << lev
                    is_q = ((tok // w) % 2) == (1 if d == 0 else 0)
                    ex = jnp.exp(sums[(1 + lev) * L:(2 + lev) * L, hs])
                    ql = jnp.where(is_q, q * ex, 0.0).astype(BF16)
                    kl = jnp.where(is_q, 0.0, k * ex).astype(BF16)
                    a = a + pair_ref[lev] * _dot_nt(ql, kl)
                o = (jnp.dot(a.astype(BF16), v[:, hs], preferred_element_type=F32)
                     + jnp.sum(q * k, axis=-1, keepdims=True) * vf[:, hs]
                     + _dot_nt((q * jnp.exp(b)).astype(BF16), st_old.astype(BF16)))
                o_refs[d][0, rows, hs] = o.astype(o_refs[d].dtype)
            kd = k * jnp.exp(b_last - b)
            s_ref[0, idx] = st_old * jnp.exp(b_last) + _tdot(vf[:, hs], kd)

    def body(c, carry):
        chunk_step(0, c)
        chunk_step(1, n_chunks - 1 - c)
        return carry

    lax.fori_loop(0, n_chunks, body, 0, unroll=2)


def _hgrn_scan(pb, pz, lb_logits, state, need_o):
    b, t, _ = pb.shape
    L, blk_t = HG_CHUNK, min(HG_BLOCK, t)
    nb = t // blk_t
    fwd = lambda col: (lambda bi, j: (bi, j, col))
    bwd = lambda col: (lambda bi, j: (bi, nb - 1 - j, col))
    blk = lambda im: pl.BlockSpec((1, blk_t, MIX_W), im)
    whole = lambda shape: pl.BlockSpec(shape, lambda bi, j: (0,) * len(shape))
    sums, pair = _hgrn_sum_masks(L)
    st = pl.BlockSpec((1, 2 * N_HEADS, HEAD_DIM, HEAD_DIM), lambda bi, j: (bi, 0, 0, 0))
    in_specs = [blk(fwd(0)), blk(bwd(0)),
                blk(fwd(1)), blk(bwd(1)),
                blk(fwd(0)), blk(bwd(1)),
                whole(lb_logits.shape), whole(sums.shape), whole(pair.shape), st]
    st_shape = jax.ShapeDtypeStruct((b, 2 * N_HEADS, HEAD_DIM, HEAD_DIM), F32)
    out_shape, out_specs = [st_shape], [st]
    if need_o:
        out_shape = [jax.ShapeDtypeStruct((b, t, MIX_W), BF16)] * 2 + out_shape
        out_specs = [blk(fwd(0)), blk(bwd(0))] + out_specs
    res = pl.pallas_call(
        functools.partial(_hgrn_kernel, chunk=L, block=blk_t, need_o=need_o),
        out_shape=out_shape,
        grid=(b, nb),
        in_specs=in_specs,
        out_specs=out_specs,
        compiler_params=pltpu.CompilerParams(dimension_semantics=("arbitrary", "arbitrary"),
                                             vmem_limit_bytes=V7X_VMEM_LIMIT),
        name="hgrn_lat" if need_o else "hgrn_ctx",
    )(pb, pb, pb, pb, pz, pz, lb_logits, sums, pair, state)
    return (res[0], res[1], res[2]) if need_o else (None, None, res[0])


def _head_norm(hsum, w):
    parts = []
    for h in range(N_HEADS):
        blk = hsum[:, h * HEAD_DIM:(h + 1) * HEAD_DIM]
        parts.append(blk * lax.rsqrt(jnp.mean(blk * blk, axis=-1, keepdims=True) + EPS))
    return jnp.concatenate(parts, axis=-1) * w


def _out_proj_kernel(x_ref, mf_ref, mb_ref, o_ref, hf_ref, hb_ref, g_ref, mw_ref, hw_ref, wout_ref,
                     g1_ref, sh2_ref, sc2_ref, n2w_ref, x1_ref, h2_ref):
    ml = _head_norm(mf_ref[0].astype(F32) + mb_ref[0].astype(F32), mw_ref[...])
    ml = ml * _sigmoid(o_ref[0].astype(F32))
    hg = _head_norm(hf_ref[0].astype(F32) + hb_ref[0].astype(F32), hw_ref[...])
    gg = g_ref[0].astype(F32)
    hg = hg * (gg * _sigmoid(gg))
    mix = (jnp.dot(ml.astype(BF16), wout_ref[0:MIX_W, :], preferred_element_type=F32)
           + jnp.dot(hg.astype(BF16), wout_ref[MIX_W:2 * MIX_W, :], preferred_element_type=F32))
    x1 = x_ref[0] + g1_ref[0] * mix
    x1_ref[0] = x1
    y = x1 * lax.rsqrt(jnp.mean(x1 * x1, axis=-1, keepdims=True) + EPS) * n2w_ref[...]
    h2_ref[0] = (y * (1.0 + sc2_ref[0]) + sh2_ref[0]).astype(BF16)


def _out_proj(x, mf, mb, pa, hf, hb, pb, ml_norm_w, hg_norm_w, w_out, g1, sh2, sc2, norm2_w, tm):
    b, t, d = x.shape
    tok = lambda w, col: pl.BlockSpec((1, tm, w), lambda bi, i: (bi, i, col))
    per_b = pl.BlockSpec((1, 1, d), lambda bi, i: (bi, 0, 0))
    whole = lambda shape: pl.BlockSpec(shape, lambda bi, i: (0,) * len(shape))
    return pl.pallas_call(
        _out_proj_kernel,
        out_shape=[jax.ShapeDtypeStruct((b, t, d), F32), jax.ShapeDtypeStruct((b, t, d), BF16)],
        grid=(b, t // tm),
        in_specs=[tok(d, 0), tok(MIX_W, 0), tok(MIX_W, 0), tok(MIX_W, 3),
                  tok(MIX_W, 0), tok(MIX_W, 0), tok(MIX_W, 2),
                  whole((1, MIX_W)), whole((1, MIX_W)), whole(w_out.shape),
                  per_b, per_b, per_b, whole((1, d))],
        out_specs=[tok(d, 0), tok(d, 0)],
        compiler_params=pltpu.CompilerParams(dimension_semantics=("arbitrary", "arbitrary"),
                                             vmem_limit_bytes=V7X_VMEM_LIMIT),
        name="out_proj",
    )(x, mf, mb, pa, hf, hb, pb, ml_norm_w, hg_norm_w, w_out, g1, sh2, sc2, norm2_w)


def _ffn_kernel(h_ref, hp_ref, hn_ref, x1_ref, g2_ref, wup_ref, cw_ref, wdn_ref, fw_ref, o_ref, *stage_refs,
                width, rows_per_blk, chan_chunk):
    i = pl.program_id(1)
    last = pl.num_programs(1) - 1
    hp = jnp.where(i > 0, hp_ref[0], jnp.zeros_like(hp_ref[0]))
    hn = jnp.where(i < last, hn_ref[0], jnp.zeros_like(hn_ref[0]))
    hin = jnp.concatenate([hp, h_ref[0], hn], axis=0)
    d_ff = wdn_ref.shape[0]
    n_chunks = d_ff // chan_chunk
    for scr in stage_refs:
        _stage_init(scr, hin.shape[0])

    def cols_of(c, half):
        return slice(half * d_ff + c * chan_chunk, half * d_ff + (c + 1) * chan_chunk)

    def up_project(c):
        for half in range(2):
            u = jnp.dot(hin, wup_ref[:, cols_of(c, half)], preferred_element_type=F32)
            _stage_store(stage_refs[2 * (c % 2) + half], u)

    def down_project(c, act):
        return jnp.dot(act, wdn_ref[c * chan_chunk:(c + 1) * chan_chunk, :], preferred_element_type=F32)

    up_project(0)
    acc = jnp.zeros((rows_per_blk * width, wdn_ref.shape[1]), F32)
    act = None
    for c in range(n_chunks):
        if c + 1 < n_chunks:
            up_project(c + 1)
        if act is not None:
            acc = acc + down_project(c - 1, act)
        a, bgate = (_conv3x3_staged(stage_refs[2 * (c % 2) + half], cw_ref[:, cols_of(c, half)],
                                    width, rows_per_blk) for half in range(2))
        act = (a * _sigmoid(a) * bgate).astype(BF16)
    acc = acc + down_project(n_chunks - 1, act)
    x2 = x1_ref[0] + g2_ref[0] * acc
    o_ref[0] = x2 * lax.rsqrt(jnp.mean(x2 * x2, axis=-1, keepdims=True) + EPS) * fw_ref[...]


def _conv_ffn(h2, x1, g2, w_up, conv_w9, w_down, final_w):
    b, t, d = x1.shape
    width, rows_per_blk = GRID_W, FFN_ROWS
    n_rows = t // width
    tm = rows_per_blk * width
    prev = pl.BlockSpec((1, width, d), lambda bi, i: (bi, jnp.maximum(i * rows_per_blk - 1, 0), 0))
    nxt = pl.BlockSpec((1, width, d),
                       lambda bi, i: (bi, jnp.minimum((i + 1) * rows_per_blk, n_rows - 1), 0))
    tok = pl.BlockSpec((1, tm, d), lambda bi, i: (bi, i, 0))
    whole = lambda shape: pl.BlockSpec(shape, lambda bi, i: (0,) * len(shape),
                                       pipeline_mode=pl.Buffered(1))
    return pl.pallas_call(
        functools.partial(_ffn_kernel, width=width, rows_per_blk=rows_per_blk, chan_chunk=FFN_CW),
        out_shape=jax.ShapeDtypeStruct((b, t, d), F32),
        grid=(b, n_rows // rows_per_blk),
        in_specs=[tok, prev, nxt, tok, pl.BlockSpec((1, 1, d), lambda bi, i: (bi, 0, 0)),
                  whole(w_up.shape), whole(conv_w9.shape), whole(w_down.shape), whole((1, d))],
        out_specs=tok,
        scratch_shapes=[pltpu.VMEM(((rows_per_blk + 2) * width + 2 * CONV_PAD, FFN_CW), F32)] * 4,
        compiler_params=pltpu.CompilerParams(dimension_semantics=("arbitrary", "arbitrary"),
                                             vmem_limit_bytes=V7X_VMEM_LIMIT),
        name="conv_ffn",
    )(h2, h2, h2, x1, g2, w_up, conv_w9, w_down, final_w)


def _reorder_w_in(w_in):
    ml, hg = 4 * MIX_W, 4 * MIX_W + 4 * N_HEADS
    cols = [w_in[:, :ml],
            w_in[:, hg:hg + MIX_W], w_in[:, hg + 3 * MIX_W:hg + 5 * MIX_W],
            w_in[:, hg + MIX_W:hg + 3 * MIX_W], _spread_gates(w_in[:, ml:hg])]
    return jnp.concatenate(cols, axis=1).astype(BF16)


def _spread_gates(a):
    groups = [jnp.pad(a[..., 4 * g:4 * g + 4], [(0, 0)] * (a.ndim - 1) + [(0, 4)]) for g in range(4)]
    out = jnp.concatenate(groups, axis=-1)
    return jnp.pad(out, [(0, 0)] * (a.ndim - 1) + [(0, GATE_PAD - GATE_ROWS)])


def kernel(x, c, ctx, c_ctx, w_mod, b_mod, norm1_w, w_in, mlstm_gate_b, mlstm_conv_w, mlstm_norm_w,
           hgrn_lb_logits, hgrn_norm_w, w_out, norm2_w, w_up, ffn_conv_w, w_down, final_norm_w):
    b, t, d = x.shape
    assert w_mod.shape[0] == 1 and hgrn_lb_logits.shape[1] == 2, "single-layer block"
    assert d == 2 * MIX_W and t % (GRID_W * FFN_ROWS) == 0

    rows = -(-(b + 1) // 8) * 8
    cc = jnp.concatenate([c, c_ctx[None, :], jnp.zeros((rows - b - 1, d), F32)], axis=0)
    mod = _modulation(cc, w_mod[0], b_mod)
    mod_l = mod[:b].reshape(b, N_MOD, 1, d)
    sh1, sc1, g1, sh2, sc2, g2 = (mod_l[:, j] for j in range(N_MOD))
    mod_c = mod[b:b + 1].reshape(1, N_MOD, 1, d)
    sh1c, sc1c = mod_c[:, 0], mod_c[:, 1]

    w_cat = _reorder_w_in(w_in[0])
    gate_b = _spread_gates(mlstm_gate_b)
    pa, pb, pz, pg = _in_proj(x, sh1, sc1, norm1_w, w_cat, gate_b, TOK_TILE)
    pa_c, pb_c, pz_c, pg_c = _in_proj(ctx, sh1c, sc1c, norm1_w, w_cat, gate_b, ctx.shape[1])

    conv9 = mlstm_conv_w[0].reshape(9, 2 * MIX_W)
    qk = _qk_conv(pa, conv9, GRID_W, True)
    qk_c = _qk_conv(pa_c, conv9, ctx.shape[1], False)

    ml_zero = (jnp.zeros((b, 2, N_HEADS, HEAD_DIM, 2 * HEAD_DIM), F32), jnp.zeros((b, 2, 8, HEAD_DIM), F32))
    _, _, ml_state = _mlstm_scan(qk_c, pa_c, pg_c, ml_zero, False)
    mf, mb, _ = _mlstm_scan(qk, pa, pg, ml_state, True)

    hg_zero = jnp.zeros((b, 2 * N_HEADS, HEAD_DIM, HEAD_DIM), F32)
    _, _, hg_state = _hgrn_scan(pb_c, pz_c, hgrn_lb_logits, hg_zero, False)
    hf, hb, _ = _hgrn_scan(pb, pz, hgrn_lb_logits, hg_state, True)

    x1, h2 = _out_proj(x, mf, mb, pa, hf, hb, pb, mlstm_norm_w, hgrn_norm_w, w_out[0].astype(BF16),
                       g1, sh2, sc2, norm2_w, TOK_TILE)
    return _conv_ffn(h2, x1, g2, w_up[0].astype(BF16), ffn_conv_w[0].reshape(9, -1),
                     w_down[0].astype(BF16), final_norm_w[None, :])
```

```python
import functools

import numpy as np
import jax
import jax.numpy as jnp
from jax import lax
from jax.experimental import pallas as pl
from jax.experimental.pallas import tpu as pltpu

F32 = jnp.float32
BF16 = jnp.bfloat16

EPS = 1e-6
GRID_W = 64
HEAD_DIM = 128
N_HEADS = 4
MIX_W = N_HEADS * HEAD_DIM
N_MOD = 6
GATE_PAD = 128
GATE_ROWS = 32
V7X_VMEM_LIMIT = 56 * 1024 * 1024

ML_CHUNK = 256
HG_BLOCK = 256
TOK_TILE = 512
FFN_ROWS = 8
FFN_CW = 256


def _sigmoid(x):
    return 1.0 / (1.0 + jnp.exp(-x))


def _log_sigmoid(x):
    return jnp.minimum(x, 0.0) - jnp.log1p(jnp.exp(-jnp.abs(x)))


def _tdot(a, b):
    return jnp.dot(a.T.astype(BF16), b.astype(BF16), preferred_element_type=F32)


def _dot_nt(a, b):
    return lax.dot_general(a, b, (((1,), (1,)), ((), ())), preferred_element_type=F32)


def _mod_kernel(c_ref, w_ref, b_ref, o_ref):
    c = c_ref[...]
    s = (c * _sigmoid(c)).astype(BF16)
    o_ref[...] = jnp.dot(s, w_ref[...].astype(BF16), preferred_element_type=F32) + b_ref[...]


def _modulation(cc, w_mod, b_mod):
    rows, d = cc.shape
    n = w_mod.shape[1]
    tn = 1024
    return pl.pallas_call(
        _mod_kernel,
        out_shape=jax.ShapeDtypeStruct((rows, n), F32),
        grid=(n // tn,),
        in_specs=[pl.BlockSpec((rows, d), lambda j: (0, 0)),
                  pl.BlockSpec((d, tn), lambda j: (0, j)),
                  pl.BlockSpec((1, tn), lambda j: (0, j))],
        out_specs=pl.BlockSpec((rows, tn), lambda j: (0, j)),
        compiler_params=pltpu.CompilerParams(dimension_semantics=("arbitrary",)),
        name="mod",
    )(cc, w_mod, b_mod)


def _in_proj_kernel(x_ref, sh_ref, sc_ref, nw_ref, w_ref, gb_ref, pa_ref, pb_ref, pz_ref, pg_ref):
    x = x_ref[0]
    ms = jnp.mean(x * x, axis=-1, keepdims=True)
    h = x * lax.rsqrt(ms + EPS) * nw_ref[...]
    h = (h * (1.0 + sc_ref[0]) + sh_ref[0]).astype(BF16)
    col = 0
    for ref, dt in ((pa_ref, BF16), (pb_ref, BF16), (pz_ref, F32)):
        width = ref.shape[-1]
        step = min(width, 512)
        for j in range(width // step):
            y = jnp.dot(h, w_ref[:, col:col + step], preferred_element_type=F32)
            ref[0, :, j * step:(j + 1) * step] = y.astype(dt)
            col += step
    g = jnp.dot(h, w_ref[:, col:col + GATE_PAD], preferred_element_type=F32) + gb_ref[...]
    pg_ref[0] = g.T[0:GATE_ROWS, :]


def _in_proj(x, sh, sc, norm_w, w_cat, gate_b, tm):
    b, t, d = x.shape
    nc = w_cat.shape[1]
    widths = (4 * MIX_W, 3 * MIX_W, 2 * MIX_W)
    assert sum(widths) + GATE_PAD == nc
    mod_idx = (lambda bi, i: (bi, 0, 0)) if sh.shape[0] > 1 else (lambda bi, i: (0, 0, 0))
    out_shape = [jax.ShapeDtypeStruct((b, t, widths[0]), BF16),
                 jax.ShapeDtypeStruct((b, t, widths[1]), BF16),
                 jax.ShapeDtypeStruct((b, t, widths[2]), F32),
                 jax.ShapeDtypeStruct((b, GATE_ROWS, t), F32)]
    return pl.pallas_call(
        _in_proj_kernel,
        out_shape=out_shape,
        grid=(b, t // tm),
        in_specs=[pl.BlockSpec((1, tm, d), lambda bi, i: (bi, i, 0)),
                  pl.BlockSpec((1, 1, d), mod_idx),
                  pl.BlockSpec((1, 1, d), mod_idx),
                  pl.BlockSpec((1, d), lambda bi, i: (0, 0)),
                  pl.BlockSpec((d, nc), lambda bi, i: (0, 0)),
                  pl.BlockSpec((1, GATE_PAD), lambda bi, i: (0, 0))],
        out_specs=[pl.BlockSpec((1, tm, w), lambda bi, i: (bi, i, 0)) for w in widths]
        + [pl.BlockSpec((1, GATE_ROWS, tm), lambda bi, i: (bi, 0, i))],
        compiler_params=pltpu.CompilerParams(dimension_semantics=("arbitrary", "arbitrary"),
                                             vmem_limit_bytes=V7X_VMEM_LIMIT),
        name="in_proj",
    )(x, sh, sc, norm_w, w_cat, gate_b)


def _conv3x3(u, w9, width, rows_out, two_d):
    ntok = u.shape[0]
    col = lax.broadcasted_iota(jnp.int32, (ntok, 1), 0) % width
    left = jnp.where(col == 0, 0.0, pltpu.roll(u, 1, 0))
    right = jnp.where(col == width - 1, 0.0, pltpu.roll(u, ntok - 1, 0))
    srcs = (left, u, right)
    acc = None
    for dr in (range(3) if two_d else (1,)):
        lo = dr * width if two_d else 0
        for dc in range(3):
            term = srcs[dc][lo:lo + rows_out * width] * w9[dr * 3 + dc:dr * 3 + dc + 1, :]
            acc = term if acc is None else acc + term
    return acc


CONV_PAD = 8


def _stage_init(scr, n_in):
    zeros = jnp.zeros((CONV_PAD, scr.shape[1]), F32)
    scr[0:CONV_PAD, :] = zeros
    scr[CONV_PAD + n_in:2 * CONV_PAD + n_in, :] = zeros


def _stage_store(scr, u):
    scr[CONV_PAD:CONV_PAD + u.shape[0], :] = u


def _conv3x3_staged(scr, w9, width, rows_out):
    n_out = rows_out * width
    col = lax.broadcasted_iota(jnp.int32, (n_out, 1), 0) % width

    def col_tap(dc):
        acc = None
        for dr in range(3):
            src = scr[CONV_PAD + dr * width + dc - 1:CONV_PAD + dr * width + dc - 1 + n_out, :]
            term = src * w9[dr * 3 + dc:dr * 3 + dc + 1, :]
            acc = term if acc is None else acc + term
        return acc

    return (col_tap(1) + jnp.where(col == 0, 0.0, col_tap(0))
            + jnp.where(col == width - 1, 0.0, col_tap(2)))


def _halo_specs(rows_per_blk, width, n_rows, chan_blk):
    prev = pl.BlockSpec((1, width, chan_blk),
                        lambda bi, i, c: (bi, jnp.maximum(i * rows_per_blk - 1, 0), c))
    nxt = pl.BlockSpec((1, width, chan_blk),
                       lambda bi, i, c: (bi, jnp.minimum((i + 1) * rows_per_blk, n_rows - 1), c))
    return prev, nxt


def _qk_conv_kernel(*refs, width, rows_per_blk, two_d):
    if two_d:
        main_ref, prev_ref, next_ref, w_ref, o_ref = refs
        i = pl.program_id(1)
        last = pl.num_programs(1) - 1
        prev = jnp.where(i > 0, prev_ref[0].astype(F32), 0.0)
        nxt = jnp.where(i < last, next_ref[0].astype(F32), 0.0)
        u = jnp.concatenate([prev, main_ref[0].astype(F32), nxt], axis=0)
    else:
        main_ref, w_ref, o_ref = refs
        u = main_ref[0].astype(F32)
    y = _conv3x3(u, w_ref[...], width, rows_per_blk, two_d)
    y = y * _sigmoid(y)
    scale = jnp.where(pl.program_id(2) == 0, HEAD_DIM ** -0.5, 1.0)
    o_ref[0] = (y * scale).astype(BF16)


def _qk_conv(pa, w9, width, two_d):
    b, t, _ = pa.shape
    cb = MIX_W
    if two_d:
        rows_per_blk = 8
        n_rows = t // width
        grid = (b, n_rows // rows_per_blk, 2)
        prev, nxt = _halo_specs(rows_per_blk, width, n_rows, cb)
        in_specs = [pl.BlockSpec((1, rows_per_blk * width, cb), lambda bi, i, c: (bi, i, c)), prev, nxt]
        args = (pa, pa, pa)
    else:
        rows_per_blk = 1
        assert t == width
        grid = (b, 1, 2)
        in_specs = [pl.BlockSpec((1, width, cb), lambda bi, i, c: (bi, 0, c))]
        args = (pa,)
    in_specs.append(pl.BlockSpec((9, cb), lambda bi, i, c: (0, c)))
    return pl.pallas_call(
        functools.partial(_qk_conv_kernel, width=width, rows_per_blk=rows_per_blk, two_d=two_d),
        out_shape=jax.ShapeDtypeStruct((b, t, 2 * MIX_W), BF16),
        grid=grid,
        in_specs=in_specs,
        out_specs=pl.BlockSpec((1, rows_per_blk * width, cb), lambda bi, i, c: (bi, i, c)),
        compiler_params=pltpu.CompilerParams(dimension_semantics=("arbitrary",) * 3,
                                             vmem_limit_bytes=V7X_VMEM_LIMIT),
        name="qk_conv",
    )(*args, w9)


def _mlstm_kernel(*refs, chunk, need_h):
    q_refs, k_refs, v_refs, g_refs = refs[0:2], refs[2:4], refs[4:6], refs[6:8]
    cum_ref, c0_ref, m0_ref = refs[8:11]
    outs = refs[11:]
    if need_h:
        h_refs, (c_ref, m_ref) = outs[0:2], outs[2:4]
    else:
        c_ref, m_ref = outs

    @pl.when(pl.program_id(1) == 0)
    def _():
        c_ref[...] = c0_ref[...]
        m_ref[...] = m0_ref[...]

    L = chunk
    t_idx = lax.broadcasted_iota(jnp.int32, (L, L), 0)
    s_idx = lax.broadcasted_iota(jnp.int32, (L, L), 1)
    ones_col = jnp.where(lax.broadcasted_iota(jnp.int32, (L, HEAD_DIM), 1) == 0, 1.0, 0.0).astype(BF16)
    for d in range(2):
        mask = (s_idx <= t_idx) if d == 0 else (s_idx >= t_idx)
        last = L - 1 if d == 0 else 0
        g = g_refs[d][0]
        gi = g[16 * d:16 * d + 8]
        lf = _log_sigmoid(g[16 * d + 8:16 * d + 16])
        b = _exact_dot_mask(lf, cum_ref[d])
        u = gi - b
        cm = _scan_max(u, reverse=(d == 1))
        b_last = b[:, last:last + 1]
        cm_last = cm[:, last:last + 1]
        m_old = m_ref[0, d][:, 0:1]
        c_stab = jnp.maximum(cm, m_old)
        a_inter = jnp.exp(m_old - c_stab)
        e_neg_mt = jnp.exp(-(b + c_stab))
        ws = jnp.exp(u - cm_last)
        m_new = jnp.maximum(b_last + m_old, b_last + cm_last)
        decay = jnp.exp(b_last + m_old - m_new)
        gain = jnp.exp(b_last + cm_last - m_new)
        m_ref[0, d] = jnp.broadcast_to(m_new, (8, HEAD_DIM))
        cols = jnp.concatenate([c_stab, a_inter, e_neg_mt, ws, jnp.zeros((HEAD_DIM - 32, L), F32)],
                               axis=0).T
        for h in range(N_HEADS):
            hs = slice(h * HEAD_DIM, (h + 1) * HEAD_DIM)
            col = lambda j: cols[:, 8 * j + h:8 * j + h + 1]
            qh = q_refs[d][0, :, hs]
            kh = k_refs[d][0, :, hs]
            v_ext = jnp.concatenate([v_refs[d][0, :, hs], ones_col], axis=1)
            cn_old = c_ref[0, d, h]
            if need_h:
                w = jnp.where(mask, jnp.exp(u[h:h + 1, :] - col(0)), 0.0)
                s = (_dot_nt(qh, kh) * w).astype(BF16)
                intra = jnp.dot(s, v_ext, preferred_element_type=F32)
                inter = jnp.dot(qh, cn_old.astype(BF16), preferred_element_type=F32)
                comb = intra + col(1) * inter
                den = comb[:, HEAD_DIM:HEAD_DIM + 1]
                hh = comb[:, 0:HEAD_DIM] / jnp.maximum(jnp.abs(den), col(2))
                h_refs[d][0, :, hs] = hh.astype(h_refs[d].dtype)
            kw = kh.astype(F32) * col(3)
            c_ref[0, d, h] = decay[h:h + 1, :] * cn_old + gain[h:h + 1, :] * _tdot(kw, v_ext)


def _exact_dot_mask(x, mask):
    hi = x.astype(BF16)
    r1 = x - hi.astype(F32)
    mid = r1.astype(BF16)
    lo = (r1 - mid.astype(F32)).astype(BF16)
    return (jnp.dot(hi, mask, preferred_element_type=F32) + jnp.dot(mid, mask, preferred_element_type=F32)
            + jnp.dot(lo, mask, preferred_element_type=F32))


def _scan_max(x, reverse):
    n = x.shape[-1]
    lane = lax.broadcasted_iota(jnp.int32, x.shape, 1)
    sh = 1
    while sh < n:
        if reverse:
            x = jnp.where(lane < n - sh, jnp.maximum(x, pltpu.roll(x, n - sh, 1)), x)
        else:
            x = jnp.where(lane >= sh, jnp.maximum(x, pltpu.roll(x, sh, 1)), x)
        sh *= 2
    return x


def _cum_masks(chunk):
    t = np.arange(chunk)
    fwd = (t[:, None] <= t[None, :]).astype(np.float32)
    return jnp.asarray(np.stack([fwd, fwd.T]), BF16)


def _mlstm_scan(qk, pa, pg, state, need_h):
    b, t, _ = qk.shape
    L = ML_CHUNK
    nc = t // L
    fwd = lambda col: (lambda bi, j: (bi, j, col))
    bwd = lambda col: (lambda bi, j: (bi, nc - 1 - j, col))
    blk = lambda im: pl.BlockSpec((1, L, MIX_W), im)
    gate = lambda im: pl.BlockSpec((1, GATE_ROWS, L), im)
    st_c = pl.BlockSpec((1, 2, N_HEADS, HEAD_DIM, 2 * HEAD_DIM), lambda bi, j: (bi, 0, 0, 0, 0))
    st_m = pl.BlockSpec((1, 2, 8, HEAD_DIM), lambda bi, j: (bi, 0, 0, 0))
    in_specs = [blk(fwd(0)), blk(bwd(0)),
                blk(fwd(1)), blk(bwd(1)),
                blk(fwd(2)), blk(bwd(2)),
                gate(lambda bi, j: (bi, 0, j)), gate(lambda bi, j: (bi, 0, nc - 1 - j)),
                pl.BlockSpec((2, L, L), lambda bi, j: (0, 0, 0)), st_c, st_m]
    out_shape = [jax.ShapeDtypeStruct(state[0].shape, F32), jax.ShapeDtypeStruct(state[1].shape, F32)]
    out_specs = [st_c, st_m]
    if need_h:
        out_shape = [jax.ShapeDtypeStruct((b, t, MIX_W), BF16)] * 2 + out_shape
        out_specs = [blk(fwd(0)), blk(bwd(0))] + out_specs
    res = pl.pallas_call(
        functools.partial(_mlstm_kernel, chunk=L, need_h=need_h),
        out_shape=out_shape,
        grid=(b, nc),
        in_specs=in_specs,
        out_specs=out_specs,
        compiler_params=pltpu.CompilerParams(dimension_semantics=("arbitrary", "arbitrary"),
                                             vmem_limit_bytes=V7X_VMEM_LIMIT),
        name="mlstm_lat" if need_h else "mlstm_ctx",
    )(qk, qk, qk, qk, pa, pa, pg, pg, _cum_masks(L), *state)
    return (res[0], res[1], tuple(res[2:])) if need_h else (None, None, tuple(res))


def _exact_mask_dot(mask, x):
    hi = x.astype(BF16)
    r1 = x - hi.astype(F32)
    mid = r1.astype(BF16)
    lo = (r1 - mid.astype(F32)).astype(BF16)
    return (jnp.dot(mask, hi, preferred_element_type=F32) + jnp.dot(mask, mid, preferred_element_type=F32)
            + jnp.dot(mask, lo, preferred_element_type=F32))


def _hgrn_gate(z, lbl):
    e = jnp.exp(lbl - jnp.max(lbl, axis=0, keepdims=True))
    lb = e[0:1, :] / jnp.sum(e, axis=0, keepdims=True)
    t = jnp.exp(-jnp.abs(z))
    r = 1.0 / (1.0 + t)
    tr = t * r
    pos = z >= 0.0
    return lb + (1.0 - lb) * jnp.where(pos, r, tr), (1.0 - lb) * jnp.where(pos, tr, r)


def _later_block(n, w):
    t = lax.broadcasted_iota(jnp.int32, (n, 1), 0)
    return ((t // w) % 2) == 1


def _pair_split(a, w):
    n, c = a.shape
    r = a.reshape(n // (2 * w), 2, w, c)
    return r[:, 0], r[:, 1]


def _pair_join(early, late):
    m, w, c = early.shape
    return jnp.stack([early, late], axis=1).reshape(2 * m * w, c)


def _level_operands(q, k, p, x, w, q_late):
    if w < 8:
        late = _later_block(q.shape[0], w)
        is_q = late if q_late else jnp.logical_not(late)
        return jnp.where(is_q, q * p, 0.0).astype(BF16), jnp.where(is_q, 0.0, k * x).astype(BF16)
    (q_e, q_l), (k_e, k_l), (p_e, p_l), (x_e, x_l) = (_pair_split(a, w) for a in (q, k, p, x))
    zero = jnp.zeros_like(q_e)
    if q_late:
        return _pair_join(zero, q_l * p_l).astype(BF16), _pair_join(k_e * x_e, zero).astype(BF16)
    return _pair_join(q_e * p_e, zero).astype(BF16), _pair_join(zero, k_l * x_l).astype(BF16)


def _double_products(p, x, tb, w, q_late):
    n = p.shape[0]
    if w < 8:
        late = _later_block(n, w)
        nb = jnp.where(late, pltpu.roll(tb, w, 0), pltpu.roll(tb, n - w, 0))
        is_q = late if q_late else jnp.logical_not(late)
        return jnp.where(is_q, p * nb, p), jnp.where(is_q, x, x * nb), tb * nb
    (p_e, p_l), (x_e, x_l), (t_e, t_l) = (_pair_split(a, w) for a in (p, x, tb))
    tt = t_e * t_l
    if q_late:
        return _pair_join(p_e, p_l * t_e), _pair_join(x_e * t_l, x_l), _pair_join(tt, tt)
    return _pair_join(p_e * t_l, p_l), _pair_join(x_e, x_l * t_e), _pair_join(tt, tt)


def _hgrn_main_kernel(q_ref, v_ref, zf_ref, zb_ref, lbl_ref, pair_ref, sf_ref, sb_ref, o_ref):
    n = q_ref.shape[1]
    n_lev = n.bit_length() - 1
    qa = q_ref[0].astype(F32)
    q = qa * _sigmoid(qa) * (HEAD_DIM ** -0.5)
    v = v_ref[0]
    f_k = [_hgrn_gate(zf_ref[0], lbl_ref[0]), _hgrn_gate(zb_ref[0], lbl_ref[1])]
    k = [f_k[0][1], f_k[1][1]]
    p = [f_k[0][0], f_k[1][0]]
    tb = list(p)
    x = [jnp.ones_like(q), jnp.ones_like(q)]
    lhs, rhs = [], []
    for lev in range(n_lev):
        w = 1 << lev
        ops = [_level_operands(q, k[d], p[d], x[d], w, q_late=(d == 0)) for d in range(2)]
        lhs.append([ops[0][0], ops[1][0]])
        rhs.append([ops[0][1], ops[1][1]])
        for d in range(2):
            p[d], x[d], tb[d] = _double_products(p[d], x[d], tb[d], w, q_late=(d == 0))
    qe = [(q * p[d]).astype(BF16) for d in range(2)]
    diag = q * (k[0] + k[1])
    for h in range(N_HEADS):
        hs = slice(h * HEAD_DIM, (h + 1) * HEAD_DIM)
        both = lambda pair: jnp.concatenate([pair[0][:, hs], pair[1][:, hs]], axis=1)
        a = pair_ref[0] * _dot_nt(both(lhs[0]), both(rhs[0]))
        for lev in range(1, n_lev):
            a = a + pair_ref[lev] * _dot_nt(both(lhs[lev]), both(rhs[lev]))
        state = jnp.concatenate([sf_ref[0, 0, h], sb_ref[0, 0, h]], axis=1)
        o = (jnp.dot(a.astype(BF16), v[:, hs], preferred_element_type=F32)
             + jnp.sum(diag[:, hs], axis=-1, keepdims=True) * v[:, hs].astype(F32)
             + _dot_nt(both(qe), state))
        o_ref[0, :, hs] = o.astype(o_ref.dtype)


def _pair_masks(n):
    t = np.arange(n)[:, None]
    s = np.arange(n)[None, :]
    n_lev = int(np.log2(n))
    return jnp.asarray(np.stack([(t >> (lev + 1)) == (s >> (lev + 1)) for lev in range(n_lev)]), F32)


def _hgrn_main(pb, pz, lb_logits, s_fwd, s_bwd):
    b, t, _ = pb.shape
    n = HG_BLOCK
    col = lambda c: pl.BlockSpec((1, n, MIX_W), lambda bi, j: (bi, j, c))
    whole = lambda shape: pl.BlockSpec(shape, lambda bi, j: (0,) * len(shape))
    st = pl.BlockSpec((1, 1, N_HEADS, HEAD_DIM, HEAD_DIM), lambda bi, j: (bi, j, 0, 0, 0))
    pair = _pair_masks(n)
    return pl.pallas_call(
        _hgrn_main_kernel,
        out_shape=jax.ShapeDtypeStruct((b, t, MIX_W), BF16),
        grid=(b, t // n),
        in_specs=[col(0), col(1), col(0), col(1), whole(lb_logits.shape), whole(pair.shape), st, st],
        out_specs=col(0),
        compiler_params=pltpu.CompilerParams(dimension_semantics=("arbitrary", "arbitrary"),
                                             vmem_limit_bytes=V7X_VMEM_LIMIT),
        name="hgrn_main",
    )(pb, pb, pz, pz, lb_logits, pair, s_fwd, s_bwd)


def _hgrn_state_kernel(*refs, record):
    v_refs, z_refs = refs[0:2], refs[2:4]
    lbl_ref, cum_ref, s0_ref = refs[4:7]
    outs = refs[7:]
    if record:
        rec_refs, s_ref = outs[0:2], outs[2]
    else:
        (s_ref,) = outs

    @pl.when(pl.program_id(1) == 0)
    def _():
        s_ref[...] = s0_ref[...]

    n = v_refs[0].shape[1]
    for d in range(2):
        last = n - 1 if d == 0 else 0
        f, k = _hgrn_gate(z_refs[d][0], lbl_ref[d])
        b = _exact_mask_dot(cum_ref[d], jnp.log(f))
        b_last = b[last:last + 1, :]
        kd = k * jnp.exp(b_last - b)
        vf = v_refs[d][0].astype(F32)
        for h in range(N_HEADS):
            idx = d * N_HEADS + h
            hs = slice(h * HEAD_DIM, (h + 1) * HEAD_DIM)
            st = s_ref[0, idx]
            if record:
                rec_refs[d][0, 0, h] = st.astype(BF16)
            s_ref[0, idx] = st * jnp.exp(b_last[:, hs]) + _tdot(vf[:, hs], kd[:, hs])


def _scan_masks(n):
    t = np.arange(n)
    fwd = (t[None, :] <= t[:, None]).astype(np.float32)
    return jnp.asarray(np.stack([fwd, fwd.T]), BF16)


def _hgrn_state(pb, pz, lb_logits, s0, record):
    b, t, _ = pb.shape
    n = min(HG_BLOCK, t)
    nb = t // n
    fwd = lambda c: (lambda bi, j: (bi, j, c))
    bwd = lambda c: (lambda bi, j: (bi, nb - 1 - j, c))
    blk = lambda im: pl.BlockSpec((1, n, MIX_W), im)
    whole = lambda shape: pl.BlockSpec(shape, lambda bi, j: (0,) * len(shape))
    st = pl.BlockSpec((1, 2 * N_HEADS, HEAD_DIM, HEAD_DIM), lambda bi, j: (bi, 0, 0, 0))
    rec_shape = jax.ShapeDtypeStruct((b, nb, N_HEADS, HEAD_DIM, HEAD_DIM), BF16)
    rec = lambda im: pl.BlockSpec((1, 1, N_HEADS, HEAD_DIM, HEAD_DIM), im)
    out_shape, out_specs = [jax.ShapeDtypeStruct(s0.shape, F32)], [st]
    if record:
        out_shape = [rec_shape, rec_shape] + out_shape
        out_specs = [rec(lambda bi, j: (bi, j, 0, 0, 0)), rec(lambda bi, j: (bi, nb - 1 - j, 0, 0, 0))] + out_specs
    cum = _scan_masks(n)
    return pl.pallas_call(
        functools.partial(_hgrn_state_kernel, record=record),
        out_shape=out_shape,
        grid=(b, nb),
        in_specs=[blk(fwd(1)), blk(bwd(1)),
                  blk(fwd(0)), blk(bwd(1)),
                  whole(lb_logits.shape), whole(cum.shape), st],
        out_specs=out_specs,
        compiler_params=pltpu.CompilerParams(dimension_semantics=("arbitrary", "arbitrary"),
                                             vmem_limit_bytes=V7X_VMEM_LIMIT),
        name="hgrn_state_lat" if record else "hgrn_state_ctx",
    )(pb, pb, pz, pz, lb_logits, cum, s0)


def _head_norm(hsum, w):
    parts = []
    for h in range(N_HEADS):
        blk = hsum[:, h * HEAD_DIM:(h + 1) * HEAD_DIM]
        parts.append(blk * lax.rsqrt(jnp.mean(blk * blk, axis=-1, keepdims=True) + EPS))
    return jnp.concatenate(parts, axis=-1) * w


def _out_proj_kernel(x_ref, mf_ref, mb_ref, o_ref, hg_ref, g_ref, mw_ref, hw_ref, wout_ref,
                     g1_ref, sh2_ref, sc2_ref, n2w_ref, x1_ref, h2_ref):
    ml = _head_norm(mf_ref[0].astype(F32) + mb_ref[0].astype(F32), mw_ref[...])
    ml = ml * _sigmoid(o_ref[0].astype(F32))
    hg = _head_norm(hg_ref[0].astype(F32), hw_ref[...])
    gg = g_ref[0].astype(F32)
    hg = hg * (gg * _sigmoid(gg))
    mix = (jnp.dot(ml.astype(BF16), wout_ref[0:MIX_W, :], preferred_element_type=F32)
           + jnp.dot(hg.astype(BF16), wout_ref[MIX_W:2 * MIX_W, :], preferred_element_type=F32))
    x1 = x_ref[0] + g1_ref[0] * mix
    x1_ref[0] = x1
    y = x1 * lax.rsqrt(jnp.mean(x1 * x1, axis=-1, keepdims=True) + EPS) * n2w_ref[...]
    h2_ref[0] = (y * (1.0 + sc2_ref[0]) + sh2_ref[0]).astype(BF16)


def _out_proj(x, mf, mb, pa, hg, pb, ml_norm_w, hg_norm_w, w_out, g1, sh2, sc2, norm2_w, tm):
    b, t, d = x.shape
    tok = lambda w, col: pl.BlockSpec((1, tm, w), lambda bi, i: (bi, i, col))
    per_b = pl.BlockSpec((1, 1, d), lambda bi, i: (bi, 0, 0))
    whole = lambda shape: pl.BlockSpec(shape, lambda bi, i: (0,) * len(shape))
    return pl.pallas_call(
        _out_proj_kernel,
        out_shape=[jax.ShapeDtypeStruct((b, t, d), F32), jax.ShapeDtypeStruct((b, t, d), BF16)],
        grid=(b, t // tm),
        in_specs=[tok(d, 0), tok(MIX_W, 0), tok(MIX_W, 0), tok(MIX_W, 3),
                  tok(MIX_W, 0), tok(MIX_W, 2),
                  whole((1, MIX_W)), whole((1, MIX_W)), whole(w_out.shape),
                  per_b, per_b, per_b, whole((1, d))],
        out_specs=[tok(d, 0), tok(d, 0)],
        compiler_params=pltpu.CompilerParams(dimension_semantics=("arbitrary", "arbitrary"),
                                             vmem_limit_bytes=V7X_VMEM_LIMIT),
        name="out_proj",
    )(x, mf, mb, pa, hg, pb, ml_norm_w, hg_norm_w, w_out, g1, sh2, sc2, norm2_w)


def _ffn_kernel(h_ref, hp_ref, hn_ref, x1_ref, g2_ref, wup_ref, cw_ref, wdn_ref, fw_ref, o_ref, *stage_refs,
                width, rows_per_blk, chan_chunk):
    i = pl.program_id(1)
    last = pl.num_programs(1) - 1
    hp = jnp.where(i > 0, hp_ref[0], jnp.zeros_like(hp_ref[0]))
    hn = jnp.where(i < last, hn_ref[0], jnp.zeros_like(hn_ref[0]))
    hin = jnp.concatenate([hp, h_ref[0], hn], axis=0)
    d_ff = wdn_ref.shape[0]
    n_chunks = d_ff // chan_chunk
    for scr in stage_refs:
        _stage_init(scr, hin.shape[0])

    def cols_of(c, half):
        return slice(half * d_ff + c * chan_chunk, half * d_ff + (c + 1) * chan_chunk)

    def up_project(c):
        for half in range(2):
            u = jnp.dot(hin, wup_ref[:, cols_of(c, half)], preferred_element_type=F32)
            _stage_store(stage_refs[2 * (c % 2) + half], u)

    def down_project(c, act):
        return jnp.dot(act, wdn_ref[c * chan_chunk:(c + 1) * chan_chunk, :], preferred_element_type=F32)

    up_project(0)
    acc = jnp.zeros((rows_per_blk * width, wdn_ref.shape[1]), F32)
    act = None
    for c in range(n_chunks):
        if c + 1 < n_chunks:
            up_project(c + 1)
        if act is not None:
            acc = acc + down_project(c - 1, act)
        a, bgate = (_conv3x3_staged(stage_refs[2 * (c % 2) + half], cw_ref[:, cols_of(c, half)],
                                    width, rows_per_blk) for half in range(2))
        act = (a * _sigmoid(a) * bgate).astype(BF16)
    acc = acc + down_project(n_chunks - 1, act)
    x2 = x1_ref[0] + g2_ref[0] * acc
    o_ref[0] = x2 * lax.rsqrt(jnp.mean(x2 * x2, axis=-1, keepdims=True) + EPS) * fw_ref[...]


def _conv_ffn(h2, x1, g2, w_up, conv_w9, w_down, final_w):
    b, t, d = x1.shape
    width, rows_per_blk = GRID_W, FFN_ROWS
    n_rows = t // width
    tm = rows_per_blk * width
    prev = pl.BlockSpec((1, width, d), lambda bi, i: (bi, jnp.maximum(i * rows_per_blk - 1, 0), 0))
    nxt = pl.BlockSpec((1, width, d),
                       lambda bi, i: (bi, jnp.minimum((i + 1) * rows_per_blk, n_rows - 1), 0))
    tok = pl.BlockSpec((1, tm, d), lambda bi, i: (bi, i, 0))
    whole = lambda shape: pl.BlockSpec(shape, lambda bi, i: (0,) * len(shape),
                                       pipeline_mode=pl.Buffered(1))
    return pl.pallas_call(
        functools.partial(_ffn_kernel, width=width, rows_per_blk=rows_per_blk, chan_chunk=FFN_CW),
        out_shape=jax.ShapeDtypeStruct((b, t, d), F32),
        grid=(b, n_rows // rows_per_blk),
        in_specs=[tok, prev, nxt, tok, pl.BlockSpec((1, 1, d), lambda bi, i: (bi, 0, 0)),
                  whole(w_up.shape), whole(conv_w9.shape), whole(w_down.shape), whole((1, d))],
        out_specs=tok,
        scratch_shapes=[pltpu.VMEM(((rows_per_blk + 2) * width + 2 * CONV_PAD, FFN_CW), F32)] * 4,
        compiler_params=pltpu.CompilerParams(dimension_semantics=("arbitrary", "arbitrary"),
                                             vmem_limit_bytes=V7X_VMEM_LIMIT),
        name="conv_ffn",
    )(h2, h2, h2, x1, g2, w_up, conv_w9, w_down, final_w)


def _reorder_w_in(w_in):
    ml, hg = 4 * MIX_W, 4 * MIX_W + 4 * N_HEADS
    cols = [w_in[:, :ml],
            w_in[:, hg:hg + MIX_W], w_in[:, hg + 3 * MIX_W:hg + 5 * MIX_W],
            w_in[:, hg + MIX_W:hg + 3 * MIX_W], _spread_gates(w_in[:, ml:hg])]
    return jnp.concatenate(cols, axis=1).astype(BF16)


def _spread_gates(a):
    groups = [jnp.pad(a[..., 4 * g:4 * g + 4], [(0, 0)] * (a.ndim - 1) + [(0, 4)]) for g in range(4)]
    out = jnp.concatenate(groups, axis=-1)
    return jnp.pad(out, [(0, 0)] * (a.ndim - 1) + [(0, GATE_PAD - GATE_ROWS)])


def kernel(x, c, ctx, c_ctx, w_mod, b_mod, norm1_w, w_in, mlstm_gate_b, mlstm_conv_w, mlstm_norm_w,
           hgrn_lb_logits, hgrn_norm_w, w_out, norm2_w, w_up, ffn_conv_w, w_down, final_norm_w):
    b, t, d = x.shape
    assert w_mod.shape[0] == 1 and hgrn_lb_logits.shape[1] == 2, "single-layer block"
    assert d == 2 * MIX_W and t % (GRID_W * FFN_ROWS) == 0

    rows = -(-(b + 1) // 8) * 8
    cc = jnp.concatenate([c, c_ctx[None, :], jnp.zeros((rows - b - 1, d), F32)], axis=0)
    mod = _modulation(cc, w_mod[0], b_mod)
    mod_l = mod[:b].reshape(b, N_MOD, 1, d)
    sh1, sc1, g1, sh2, sc2, g2 = (mod_l[:, j] for j in range(N_MOD))
    mod_c = mod[b:b + 1].reshape(1, N_MOD, 1, d)
    sh1c, sc1c = mod_c[:, 0], mod_c[:, 1]

    w_cat = _reorder_w_in(w_in[0])
    gate_b = _spread_gates(mlstm_gate_b)
    pa, pb, pz, pg = _in_proj(x, sh1, sc1, norm1_w, w_cat, gate_b, TOK_TILE)
    pa_c, pb_c, pz_c, pg_c = _in_proj(ctx, sh1c, sc1c, norm1_w, w_cat, gate_b, ctx.shape[1])

    conv9 = mlstm_conv_w[0].reshape(9, 2 * MIX_W)
    qk = _qk_conv(pa, conv9, GRID_W, True)
    qk_c = _qk_conv(pa_c, conv9, ctx.shape[1], False)

    ml_zero = (jnp.zeros((b, 2, N_HEADS, HEAD_DIM, 2 * HEAD_DIM), F32), jnp.zeros((b, 2, 8, HEAD_DIM), F32))
    _, _, ml_state = _mlstm_scan(qk_c, pa_c, pg_c, ml_zero, False)
    mf, mb, _ = _mlstm_scan(qk, pa, pg, ml_state, True)

    hg_zero = jnp.zeros((b, 2 * N_HEADS, HEAD_DIM, HEAD_DIM), F32)
    (hg_init,) = _hgrn_state(pb_c, pz_c, hgrn_lb_logits, hg_zero, False)
    s_fwd, s_bwd, _ = _hgrn_state(pb, pz, hgrn_lb_logits, hg_init, True)
    hg = _hgrn_main(pb, pz, hgrn_lb_logits, s_fwd, s_bwd)

    x1, h2 = _out_proj(x, mf, mb, pa, hg, pb, mlstm_norm_w, hgrn_norm_w, w_out[0].astype(BF16),
                       g1, sh2, sc2, norm2_w, TOK_TILE)
    return _conv_ffn(h2, x1, g2, w_up[0].astype(BF16), ffn_conv_w[0].reshape(9, -1),
                     w_down[0].astype(BF16), final_norm_w[None, :])
```
